```python
import math
import jax, jax.numpy as jnp
from jax import lax
import numpy as np

D_MODEL = 2048
BATCH = 2
SEQ = 4096
DEPTH = 4
DEC_BATCH = 8
DEC_SEQ = 8
PAST_LEN = 16384
PAGE_SIZE = 128

N_BRANCHES = 3
BRANCH_WIDTH = D_MODEL // 2
ATT_DH = 64
ATT_VD = 2 * ATT_DH
ATT_HEADS = BRANCH_WIDTH // ATT_VD
ATT_QK = ATT_HEADS * 2 * ATT_DH
ATT_QBLOCK = 128
REL_BUCKETS = 32
REL_MAX_DIST = 128
RWKV_DH = 64
RWKV_HEADS = BRANCH_WIDTH // RWKV_DH
RWKV_DECAY_LORA = max(32, int(round(1.8 * BRANCH_WIDTH ** 0.5 / 32)) * 32)
RWKV_AAA_LORA = max(32, int(round(1.8 * BRANCH_WIDTH ** 0.5 / 32)) * 32)
RWKV_GATE_LORA = max(32, int(round(0.6 * BRANCH_WIDTH ** 0.8 / 32)) * 32)
RWKV_LN_EPS = 64e-5
SSM_HEAD_DIM = 64
SSM_HEADS = BRANCH_WIDTH // SSM_HEAD_DIM
SSM_GROUPS = 4
SSM_STATE = 128
SSM_CONV = 4
SSM_CONV_DIM = BRANCH_WIDTH + 2 * SSM_GROUPS * SSM_STATE
SSM_CHUNK = 128
PEER_HEADS = 8
PEER_NKEYS = 128
PEER_EXPERTS = PEER_NKEYS * PEER_NKEYS
PEER_KEY_DIM = 256
PEER_HALF = PEER_KEY_DIM // 2
PEER_TOPK = 16
PEER_TOKEN_BLOCK = 128
ATT_Q_OFF = 0
ATT_K_OFF = ATT_QK
ATT_V_OFF = 2 * ATT_QK
RWKV_OFF = ATT_V_OFF + ATT_HEADS * ATT_VD
RWKV_COLS = 3 * BRANCH_WIDTH + RWKV_DECAY_LORA + RWKV_AAA_LORA + RWKV_GATE_LORA
SSM_OFF = RWKV_OFF + RWKV_COLS
SSM_COLS = BRANCH_WIDTH + SSM_CONV_DIM + SSM_HEADS
GATE_OFF = SSM_OFF + SSM_COLS
IN_COLS = GATE_OFF + N_BRANCHES * D_MODEL

kernel_name = 'hybrid_diffattn_rwkv7_mamba2_peer_step'


def _rmsnorm(x, g, eps=1e-6):
    xf = x.astype(jnp.float32)
    y = xf * lax.rsqrt(jnp.mean(xf * xf, axis=-1, keepdims=True) + eps)
    return (y * g.astype(jnp.float32)).astype(x.dtype)


def _rel_bucket(dist):
    n = jnp.maximum(dist, 0)
    max_exact = REL_BUCKETS // 2
    nf = jnp.maximum(n, 1).astype(jnp.float32)
    large = max_exact + (jnp.log(nf / max_exact) / math.log(REL_MAX_DIST / max_exact) * (REL_BUCKETS - max_exact)).astype(jnp.int32)
    large = jnp.minimum(large, REL_BUCKETS - 1)
    return jnp.where(n < max_exact, n, large)


def _diff_attention(q, k, v, q_off, lam, rel_bias):
    bsz, tq = q.shape[:2]
    tk = k.shape[1]
    kh = k.reshape(bsz, tk, ATT_HEADS, 2, ATT_DH)
    qb = min(ATT_QBLOCK, tq)
    nb = tq // qb
    q_blocks = jnp.moveaxis(q.reshape(bsz, nb, qb, ATT_HEADS, 2, ATT_DH), 1, 0)
    k_pos = jnp.arange(tk, dtype=jnp.int32)
    scale = ATT_DH ** -0.5

    def block(args):
        q_blk, n = args
        q_pos = q_off + n * qb + jnp.arange(qb, dtype=jnp.int32)
        dist = q_pos[:, None] - k_pos[None, :]
        bias = jnp.moveaxis(rel_bias[_rel_bucket(dist)].astype(jnp.float32), -1, 0)
        s = jnp.einsum('bqhmd,bkhmd->bhmqk', q_blk, kh).astype(jnp.float32) * scale + bias[None, :, None]
        s = jnp.where(dist >= 0, s, -jnp.inf)
        p = jax.nn.softmax(s, axis=-1)
        w = p[:, :, 0] - lam * p[:, :, 1]
        return jnp.einsum('bhqk,bkhe->bqhe', w.astype(v.dtype), v)

    out = lax.map(block, (q_blocks, jnp.arange(nb, dtype=jnp.int32)))
    return jnp.moveaxis(out, 0, 1).reshape(bsz, tq, ATT_HEADS, ATT_VD)


def _rwkv7(p, shift0, wkv0, l, P):
    bsz, t, _ = p.shape
    bw = BRANCH_WIDTH
    f32 = jnp.float32
    prev = jnp.concatenate([shift0[:, None].astype(p.dtype), p[:, :-1]], axis=1)
    pm = p + (prev - p) * P['rwkv_mu'][l]
    o1 = 3 * bw
    o2 = o1 + RWKV_DECAY_LORA
    o3 = o2 + RWKV_AAA_LORA
    r, k, v = pm[..., :bw], pm[..., bw:2 * bw], pm[..., 2 * bw:o1]
    w_lo, a_lo, g_lo = pm[..., o1:o2], pm[..., o2:o3], pm[..., o3:]
    w = -jax.nn.softplus(-(P['rwkv_w0'][l] + jnp.tanh(w_lo) @ P['rwkv_w2'][l])) - 0.5
    decay = jnp.exp(-jnp.exp(w.astype(f32)))
    a = jax.nn.sigmoid(P['rwkv_a0'][l] + a_lo @ P['rwkv_a2'][l])
    g = jax.nn.sigmoid(g_lo) @ P['rwkv_g2'][l]
    heads = lambda z: z.reshape(bsz, t, RWKV_HEADS, RWKV_DH).astype(f32)
    kk = heads(k * P['rwkv_kk'][l])
    kk = kk / jnp.maximum(jnp.sqrt(jnp.sum(kk * kk, axis=-1, keepdims=True)), 1e-12)
    k = k * (1 + (a - 1) * P['rwkv_ka'][l])
    rh, kh, vh, ah = heads(r), heads(k), heads(v), heads(a)

    def step(S, inp):
        r_t, w_t, k_t, v_t, a_t, b_t = inp
        sa = jnp.einsum('bhij,bhj->bhi', S, a_t)
        S = S * w_t[:, :, None, :] + sa[..., None] * b_t[:, :, None, :] + v_t[..., None] * k_t[:, :, None, :]
        return S, jnp.einsum('bhij,bhj->bhi', S, r_t)

    xs = tuple(jnp.moveaxis(z, 1, 0) for z in (rh, heads(decay), kh, vh, -kk, kk * ah))
    S_T, y = lax.scan(step, wkv0.astype(f32), xs)
    y = jnp.moveaxis(y, 0, 1)
    mu = jnp.mean(y, axis=-1, keepdims=True)
    var = jnp.mean(jnp.square(y - mu), axis=-1, keepdims=True)
    y = ((y - mu) * lax.rsqrt(var + RWKV_LN_EPS)).reshape(bsz, t, bw)
    y = y * P['rwkv_lnx_g'][l].astype(f32) + P['rwkv_lnx_b'][l].astype(f32)
    bonus = jnp.sum(rh * kh * P['rwkv_rk'][l].astype(f32), axis=-1, keepdims=True) * vh
    y = (y + bonus.reshape(bsz, t, bw)) * g.astype(f32)
    return y.astype(p.dtype), p[:, -1], S_T.astype(wkv0.dtype)


def _segsum(a):
    L = a.shape[-1]
    rep = jnp.broadcast_to(a[..., :, None], a.shape + (L,))
    s = jnp.cumsum(jnp.where(jnp.tril(jnp.ones((L, L), bool), -1), rep, 0.0), axis=-2)
    return jnp.where(jnp.tril(jnp.ones((L, L), bool)), s, -jnp.inf)


def _ssd(x, dt, A, Bm, Cm, h0):
    b, t = x.shape[:2]
    L = min(SSM_CHUNK, t)
    nc = -(-t // L)
    pad = nc * L - t
    if pad:
        padt = lambda z: jnp.pad(z, [(0, 0), (0, pad)] + [(0, 0)] * (z.ndim - 2))
        x, dt, Bm, Cm = padt(x), padt(dt), padt(Bm), padt(Cm)
    g, e, pd = x.shape[2:]
    n = Bm.shape[-1]
    xc = (x * dt[..., None]).reshape(b, nc, L, g, e, pd)
    Bc = Bm.reshape(b, nc, L, g, n)
    Cc = Cm.reshape(b, nc, L, g, n)
    a = jnp.transpose((dt * A).reshape(b, nc, L, g, e), (0, 3, 4, 1, 2))
    a_cs = jnp.cumsum(a, axis=-1)
    cb = jnp.einsum('bclgn,bcsgn->bgcls', Cc, Bc)
    wmat = cb[:, :, None] * jnp.exp(_segsum(a))
    y_diag = jnp.einsum('bgecls,bcsgep->bclgep', wmat, xc)
    decay_states = jnp.exp(a_cs[..., -1:] - a_cs)
    states = jnp.einsum('bclgn,bgecl,bclgep->bcgepn', Bc, decay_states, xc)
    states = jnp.concatenate([h0[:, None], states], axis=1)
    chunk_decay = jnp.exp(_segsum(jnp.pad(a_cs[..., -1], ((0, 0), (0, 0), (0, 0), (1, 0)))))
    states = jnp.einsum('bgezc,bcgepn->bzgepn', chunk_decay, states)
    y_off = jnp.einsum('bclgn,bcgepn,bgecl->bclgep', Cc, states[:, :-1], jnp.exp(a_cs))
    y = (y_diag + y_off).reshape(b, nc * L, g, e, pd)[:, :t]
    return y, states[:, -1]


def _mamba2(p, conv0, ssm0, l, P):
    bsz, t, _ = p.shape
    bw = BRANCH_WIDTH
    f32 = jnp.float32
    gn = SSM_GROUPS * SSM_STATE
    hpg = SSM_HEADS // SSM_GROUPS
    z = p[..., :bw]
    xbc = p[..., bw:bw + SSM_CONV_DIM]
    dt_raw = p[..., bw + SSM_CONV_DIM:]
    xpad = jnp.concatenate([conv0.astype(p.dtype), xbc], axis=1)
    cw = P['ssm_conv_w'][l]
    conv = P['ssm_conv_b'][l] + sum(xpad[:, i:i + t] * cw[i] for i in range(SSM_CONV))
    conv = jax.nn.silu(conv).astype(f32)
    xs = conv[..., :bw].reshape(bsz, t, SSM_GROUPS, hpg, SSM_HEAD_DIM)
    Bm = conv[..., bw:bw + gn].reshape(bsz, t, SSM_GROUPS, SSM_STATE)
    Cm = conv[..., bw + gn:].reshape(bsz, t, SSM_GROUPS, SSM_STATE)
    dt = jax.nn.softplus(dt_raw.astype(f32) + P['ssm_dt_bias'][l].astype(f32)).reshape(bsz, t, SSM_GROUPS, hpg)
    A = -jnp.exp(P['ssm_a_log'][l].astype(f32)).reshape(SSM_GROUPS, hpg)
    h0 = ssm0.astype(f32).reshape(bsz, SSM_GROUPS, hpg, SSM_HEAD_DIM, SSM_STATE)
    y, hT = _ssd(xs, dt, A, Bm, Cm, h0)
    y = y + P['ssm_d'][l].astype(f32).reshape(SSM_GROUPS, hpg)[:, :, None] * xs
    y = y.reshape(bsz, t, bw) * jax.nn.silu(z.astype(f32))
    yg = y.reshape(bsz, t, SSM_GROUPS, bw // SSM_GROUPS)
    yg = yg * lax.rsqrt(jnp.mean(yg * yg, axis=-1, keepdims=True) + 1e-5)
    y = yg.reshape(bsz, t, bw) * P['ssm_norm'][l].astype(f32)
    return y.astype(p.dtype), xpad[:, -(SSM_CONV - 1):], hT.reshape(bsz, SSM_HEADS, SSM_HEAD_DIM, SSM_STATE).astype(ssm0.dtype)


def _peer(x, l, P):
    bsz, t, d = x.shape
    n = bsz * t
    tb = min(PEER_TOKEN_BLOCK, n)
    nb = -(-n // tb)
    xt = jnp.pad(x.reshape(n, d), ((0, nb * tb - n), (0, 0))).reshape(nb, tb, d)
    wq, k1, k2 = P['peer_wq'][l], P['peer_k1'][l], P['peer_k2'][l]
    U, V = P['peer_u'][l], P['peer_v'][l]

    def block(xb):
        q = (xb @ wq).reshape(tb, PEER_HEADS, 2, PEER_HALF)
        s1 = jnp.einsum('thd,kd->thk', q[:, :, 0], k1).astype(jnp.float32)
        s2 = jnp.einsum('thd,kd->thk', q[:, :, 1], k2).astype(jnp.float32)
        v1, i1 = lax.top_k(s1, PEER_TOPK)
        v2, i2 = lax.top_k(s2, PEER_TOPK)
        cand = (v1[..., :, None] + v2[..., None, :]).reshape(tb, PEER_HEADS, PEER_TOPK * PEER_TOPK)
        sc, ci = lax.top_k(cand, PEER_TOPK)
        e = jnp.take_along_axis(i1, ci // PEER_TOPK, -1) * PEER_NKEYS + jnp.take_along_axis(i2, ci % PEER_TOPK, -1)
        gate = jax.nn.softmax(sc, axis=-1)
        act = jax.nn.gelu(jnp.einsum('thkd,td->thk', U[e], xb).astype(jnp.float32), approximate=False)
        return jnp.einsum('thk,thkd->td', (gate * act).astype(x.dtype), V[e])

    out = lax.map(block, xt).reshape(nb * tb, d)[:n]
    return out.reshape(bsz, t, d)


def _layer(l, x, c, k_past, v_past, shift0, wkv0, conv0, ssm0, P):
    bsz, t, _ = x.shape
    mod = jax.nn.silu(c) @ P['w_ada'][l] + P['b_ada'][l]
    sh1, sc1, g1, sh2, sc2, g2 = jnp.split(mod[:, None, :], 6, axis=-1)
    h = _rmsnorm(x, P['norm1'][l]) * (1 + sc1) + sh1
    p = h @ P['w_in'][l]
    q = p[..., ATT_Q_OFF:ATT_K_OFF].reshape(bsz, t, ATT_HEADS, 2, ATT_DH)
    k_new = p[..., ATT_K_OFF:ATT_V_OFF].reshape(bsz, t, ATT_HEADS, 2 * ATT_DH)
    v_new = p[..., ATT_V_OFF:RWKV_OFF].reshape(bsz, t, ATT_HEADS, ATT_VD)
    if k_past is None:
        k_all, v_all, q_off = k_new, v_new, 0
    else:
        k_all = jnp.concatenate([k_past.astype(k_new.dtype), k_new], axis=1)
        v_all = jnp.concatenate([v_past.astype(v_new.dtype), v_new], axis=1)
        q_off = k_past.shape[1]
    lam_init = 0.8 - 0.6 * math.exp(-0.3 * l)
    lv = P['att_lambda'][l].astype(jnp.float32)
    lam = jnp.exp(jnp.sum(lv[0] * lv[1])) - jnp.exp(jnp.sum(lv[2] * lv[3])) + lam_init
    o_att = _diff_attention(q, k_all, v_all, q_off, lam, P['rel_bias'])
    o_att = (_rmsnorm(o_att, P['att_subln'][l], 1e-5) * (1 - lam_init)).reshape(bsz, t, BRANCH_WIDTH)
    o_rwkv, shift1, wkv1 = _rwkv7(p[..., RWKV_OFF:SSM_OFF], shift0, wkv0, l, P)
    o_ssm, conv1, ssm1 = _mamba2(p[..., SSM_OFF:GATE_OFF], conv0, ssm0, l, P)
    gates = jax.nn.sigmoid(p[..., GATE_OFF:].reshape(bsz, t, N_BRANCHES, D_MODEL))
    branches = jnp.stack([o_att, o_rwkv, o_ssm], axis=2)
    proj = jnp.einsum('btie,ied->btid', branches, P['w_branch'][l])
    merged = jnp.sum(gates * proj, axis=2)
    x = x + g1 * (merged @ P['w_o'][l])
    h2 = _rmsnorm(x, P['norm2'][l]) * (1 + sc2) + sh2
    x = x + g2 * _peer(h2, l, P)
    return x, k_new, v_new, shift1, wkv1, conv1, ssm1


def setup_inputs(seed: int = 0) -> dict:
    key = jax.random.key(seed)
    ks = jax.random.split(key, 48)
    f32 = jnp.float32

    def nrm(i, shape, scale):
        return jax.random.normal(ks[i], shape, f32) * scale

    n_pages = PAST_LEN // PAGE_SIZE
    n_used = DEC_BATCH * n_pages
    n_pool = n_used + max(1, n_used // 4)
    D = D_MODEL
    BW = BRANCH_WIDTH
    ramp = jnp.linspace(0.0, 1.0, BW, dtype=f32) ** 0.85
    dt0 = jnp.exp(jax.random.uniform(ks[30], (DEPTH, SSM_HEADS), f32, math.log(1e-3), math.log(1e-1)))
    return {
        'x_prompt': nrm(0, (BATCH, SEQ, D), 1.0),
        'x_sample': nrm(1, (DEC_BATCH, DEC_SEQ, D), 1.0),
        'cache_k': nrm(2, (n_pool, DEPTH, PAGE_SIZE, ATT_HEADS, 2 * ATT_DH), 1.0),
        'cache_v': nrm(3, (n_pool, DEPTH, PAGE_SIZE, ATT_HEADS, ATT_VD), 1.0),
        'state_rwkv_shift': nrm(4, (DEPTH, DEC_BATCH, RWKV_COLS), 1.0),
        'state_rwkv': nrm(5, (DEPTH, DEC_BATCH, RWKV_HEADS, RWKV_DH, RWKV_DH), 0.5),
        'state_conv': nrm(6, (DEPTH, DEC_BATCH, SSM_CONV - 1, SSM_CONV_DIM), 1.0),
        'state_ssm': nrm(7, (DEPTH, DEC_BATCH, SSM_HEADS, SSM_HEAD_DIM, SSM_STATE), 0.5),
        'page_table': jax.random.permutation(ks[8], n_pool)[:n_used].reshape(DEC_BATCH, n_pages).astype(jnp.int32),
        'c_prompt': nrm(9, (BATCH, D), 1.0),
        'c_sample': nrm(10, (DEC_BATCH, D), 1.0),
        'w_ada': nrm(11, (DEPTH, D, 6 * D), 0.5 * D ** -0.5),
        'b_ada': nrm(12, (DEPTH, 6 * D), 0.02),
        'norm1': 1.0 + nrm(13, (DEPTH, D), 0.02),
        'norm2': 1.0 + nrm(14, (DEPTH, D), 0.02),
        'norm_f': 1.0 + nrm(15, (D,), 0.02),
        'w_in': nrm(16, (DEPTH, D, IN_COLS), D ** -0.5),
        'rel_bias': nrm(17, (REL_BUCKETS, ATT_HEADS), 0.5),
        'att_lambda': nrm(18, (DEPTH, 4, ATT_DH), 0.1),
        'att_subln': 1.0 + nrm(19, (DEPTH, ATT_VD), 0.02),
        'rwkv_mu': jax.random.uniform(ks[20], (DEPTH, RWKV_COLS), f32),
        'rwkv_w0': -6.5 + 5.0 * ramp + nrm(21, (DEPTH, BW), 0.1),
        'rwkv_w2': nrm(22, (DEPTH, RWKV_DECAY_LORA, BW), 0.1),
        'rwkv_a0': nrm(23, (DEPTH, BW), 0.1),
        'rwkv_a2': nrm(24, (DEPTH, RWKV_AAA_LORA, BW), RWKV_AAA_LORA ** -0.5),
        'rwkv_g2': nrm(25, (DEPTH, RWKV_GATE_LORA, BW), RWKV_GATE_LORA ** -0.5),
        'rwkv_kk': 0.85 + nrm(26, (DEPTH, BW), 0.02),
        'rwkv_ka': 1.0 + nrm(27, (DEPTH, BW), 0.02),
        'rwkv_rk': nrm(28, (DEPTH, RWKV_HEADS, RWKV_DH), 0.1),
        'rwkv_lnx_g': 1.0 + nrm(29, (DEPTH, BW), 0.02),
        'rwkv_lnx_b': nrm(31, (DEPTH, BW), 0.02),
        'ssm_conv_w': nrm(32, (DEPTH, SSM_CONV, SSM_CONV_DIM), SSM_CONV ** -0.5),
        'ssm_conv_b': nrm(33, (DEPTH, SSM_CONV_DIM), 0.02),
        'ssm_dt_bias': dt0 + jnp.log(-jnp.expm1(-dt0)),
        'ssm_a_log': jnp.log(jax.random.uniform(ks[34], (DEPTH, SSM_HEADS), f32, 1.0, 16.0)),
        'ssm_d': 1.0 + nrm(35, (DEPTH, SSM_HEADS), 0.1),
        'ssm_norm': 1.0 + nrm(36, (DEPTH, BW), 0.02),
        'w_branch': nrm(37, (DEPTH, N_BRANCHES, BW, D), BW ** -0.5),
        'w_o': nrm(38, (DEPTH, D, D), D ** -0.5),
        'peer_wq': nrm(39, (DEPTH, D, PEER_HEADS * PEER_KEY_DIM), D ** -0.5),
        'peer_k1': nrm(40, (DEPTH, PEER_NKEYS, PEER_HALF), PEER_HALF ** -0.5),
        'peer_k2': nrm(41, (DEPTH, PEER_NKEYS, PEER_HALF), PEER_HALF ** -0.5),
        'peer_u': nrm(42, (DEPTH, PEER_EXPERTS, D), D ** -0.5),
        'peer_v': nrm(43, (DEPTH, PEER_EXPERTS, D), PEER_HEADS ** -0.5),
    }


def reference(x_prompt, x_sample, cache_k, cache_v, state_rwkv_shift, state_rwkv, state_conv, state_ssm, page_table, c_prompt, c_sample, w_ada, b_ada, norm1, norm2, norm_f, w_in, rel_bias, att_lambda, att_subln, rwkv_mu, rwkv_w0, rwkv_w2, rwkv_a0, rwkv_a2, rwkv_g2, rwkv_kk, rwkv_ka, rwkv_rk, rwkv_lnx_g, rwkv_lnx_b, ssm_conv_w, ssm_conv_b, ssm_dt_bias, ssm_a_log, ssm_d, ssm_norm, w_branch, w_o, peer_wq, peer_k1, peer_k2, peer_u, peer_v):
    P = dict(w_ada=w_ada, b_ada=b_ada, norm1=norm1, norm2=norm2, w_in=w_in, rel_bias=rel_bias,
             att_lambda=att_lambda, att_subln=att_subln, rwkv_mu=rwkv_mu, rwkv_w0=rwkv_w0,
             rwkv_w2=rwkv_w2, rwkv_a0=rwkv_a0, rwkv_a2=rwkv_a2, rwkv_g2=rwkv_g2, rwkv_kk=rwkv_kk,
             rwkv_ka=rwkv_ka, rwkv_rk=rwkv_rk, rwkv_lnx_g=rwkv_lnx_g, rwkv_lnx_b=rwkv_lnx_b,
             ssm_conv_w=ssm_conv_w, ssm_conv_b=ssm_conv_b, ssm_dt_bias=ssm_dt_bias,
             ssm_a_log=ssm_a_log, ssm_d=ssm_d, ssm_norm=ssm_norm, w_branch=w_branch, w_o=w_o,
             peer_wq=peer_wq, peer_k1=peer_k1, peer_k2=peer_k2, peer_u=peer_u, peer_v=peer_v)
    bp = x_prompt.shape[0]
    bs = x_sample.shape[0]
    past = page_table.shape[1] * PAGE_SIZE
    dtp = x_prompt.dtype
    shift_z = jnp.zeros((bp, RWKV_COLS), dtp)
    wkv_z = jnp.zeros((bp, RWKV_HEADS, RWKV_DH, RWKV_DH), dtp)
    conv_z = jnp.zeros((bp, SSM_CONV - 1, SSM_CONV_DIM), dtp)
    ssm_z = jnp.zeros((bp, SSM_HEADS, SSM_HEAD_DIM, SSM_STATE), dtp)
    xp, xs = x_prompt, x_sample
    outs_p = ([], [], [], [], [], [])
    outs_s = ([], [], [], [], [], [])
    for l in range(DEPTH):
        xp, *new_p = _layer(l, xp, c_prompt, None, None, shift_z, wkv_z, conv_z, ssm_z, P)
        k_past = cache_k[page_table, l].reshape(bs, past, ATT_HEADS, 2 * ATT_DH)
        v_past = cache_v[page_table, l].reshape(bs, past, ATT_HEADS, ATT_VD)
        xs, *new_s = _layer(l, xs, c_sample, k_past, v_past, state_rwkv_shift[l], state_rwkv[l],
                            state_conv[l], state_ssm[l], P)
        for lst, val in zip(outs_p, new_p):
            lst.append(val)
        for lst, val in zip(outs_s, new_s):
            lst.append(val)
    y_prompt = _rmsnorm(xp, norm_f)
    y_sample = _rmsnorm(xs, norm_f)
    k_prompt = jnp.stack(outs_p[0], axis=1)
    v_prompt = jnp.stack(outs_p[1], axis=1)
    shift_prompt = jnp.stack(outs_p[2], axis=0)
    wkv_prompt = jnp.stack(outs_p[3], axis=0)
    conv_prompt = jnp.stack(outs_p[4], axis=0)
    ssm_prompt = jnp.stack(outs_p[5], axis=0)
    k_sample = jnp.stack(outs_s[0], axis=1)
    v_sample = jnp.stack(outs_s[1], axis=1)
    shift_sample = jnp.stack(outs_s[2], axis=0)
    wkv_sample = jnp.stack(outs_s[3], axis=0)
    conv_sample = jnp.stack(outs_s[4], axis=0)
    ssm_sample = jnp.stack(outs_s[5], axis=0)
    return (y_prompt, y_sample, k_prompt, v_prompt, shift_prompt, wkv_prompt, conv_prompt, ssm_prompt, k_sample, v_sample, shift_sample, wkv_sample, conv_sample, ssm_sample)
```

```python
import functools
import math

import jax
import jax.numpy as jnp
import numpy as np
from jax import lax
from jax.experimental import pallas as pl
from jax.experimental.pallas import tpu as pltpu

F32 = jnp.float32
BF16 = jnp.bfloat16

LANES = 128
SUBLANES = 8
VMEM_LIMIT_BYTES = 56 * 1024 * 1024

D_MODEL = 2048
DEPTH = 4
PAGE_SIZE = 128
BRANCH_WIDTH = D_MODEL // 2
N_BRANCHES = 3
ATT_DH = 64
ATT_VD = 2 * ATT_DH
ATT_HEADS = BRANCH_WIDTH // ATT_VD
ATT_QK = ATT_HEADS * 2 * ATT_DH
REL_BUCKETS = 32
REL_MAX_DIST = 128
RWKV_DH = 64
RWKV_HEADS = BRANCH_WIDTH // RWKV_DH
RWKV_DECAY_LORA = 64
RWKV_AAA_LORA = 64
RWKV_GATE_LORA = 160
RWKV_LN_EPS = 64e-5
RWKV_COLS = 3 * BRANCH_WIDTH + RWKV_DECAY_LORA + RWKV_AAA_LORA + RWKV_GATE_LORA
RWKV_COLS_PAD = 3456
SSM_HEAD_DIM = 64
SSM_HEADS = BRANCH_WIDTH // SSM_HEAD_DIM
SSM_GROUPS = 4
SSM_STATE = 128
SSM_CONV = 4
SSM_CONV_DIM = BRANCH_WIDTH + 2 * SSM_GROUPS * SSM_STATE
SSM_CHUNK = 128
SSM_COLS = BRANCH_WIDTH + SSM_CONV_DIM + SSM_HEADS
SSM_COLS_PAD = 3200
PEER_HEADS = 8
PEER_NKEYS = 128
PEER_EXPERTS = PEER_NKEYS * PEER_NKEYS
PEER_KEY_DIM = 256
PEER_HALF = PEER_KEY_DIM // 2
PEER_TOPK = 16
ATT_Q_OFF = 0
ATT_K_OFF = ATT_QK
ATT_V_OFF = 2 * ATT_QK
RWKV_OFF = ATT_V_OFF + ATT_HEADS * ATT_VD
SSM_OFF = RWKV_OFF + RWKV_COLS
GATE_OFF = SSM_OFF + SSM_COLS

NEG_BIG = -1e30


def _cparams(semantics):
    return pltpu.CompilerParams(dimension_semantics=semantics, vmem_limit_bytes=VMEM_LIMIT_BYTES)


def _dot(a, b):
    return jnp.dot(a, b, preferred_element_type=F32)


def _dot_nt(a, b):
    return lax.dot_general(a, b, (((1,), (1,)), ((), ())), preferred_element_type=F32)


def _dot_tn(a, b):
    return lax.dot_general(a, b, (((0,), (0,)), ((), ())), preferred_element_type=F32)


def _split2(x):
    hi = x.astype(BF16)
    lo = (x - hi.astype(F32)).astype(BF16)
    return hi, lo


def _split3(x):
    hi = x.astype(BF16)
    r = x - hi.astype(F32)
    mid = r.astype(BF16)
    lo = (r - mid.astype(F32)).astype(BF16)
    return hi, mid, lo


def _dot_x3(x, w_bf16):
    hi, mid, lo = _split3(x)
    return _dot(hi, w_bf16) + _dot(mid, w_bf16) + _dot(lo, w_bf16)


def _dot_x2(x, w_bf16):
    hi, lo = _split2(x)
    return _dot(hi, w_bf16) + _dot(lo, w_bf16)


def _block_ones(n, width):
    r = np.arange(n) // width
    return jnp.asarray((r[:, None] == r[None, :]).astype(np.float32), dtype=BF16)


def _segsum_bcast(x, ones_ref, width_block):
    cols = x.shape[1]
    outs = []
    e = ones_ref[...]
    for c in range(cols // width_block):
        outs.append(_dot_x2(x[:, c * width_block:(c + 1) * width_block], e))
    return outs[0] if len(outs) == 1 else jnp.concatenate(outs, axis=1)


def _silu(x):
    return x * jax.nn.sigmoid(x)


def _softplus(x):
    return jnp.maximum(x, 0.0) + jnp.log1p(jnp.exp(-jnp.abs(x)))


def _ada_kernel(c_ref, w_ref, b_ref, o_ref):
    c = c_ref[...]
    a = _silu(c).astype(BF16)
    o_ref[...] = _dot(a, w_ref[...].astype(BF16)) + b_ref[...]


def _ada(c_all, w_ada, b_ada):
    rows = c_all.shape[0]
    tn = 1024
    ncol = w_ada.shape[2]
    return pl.pallas_call(
        _ada_kernel,
        grid=(DEPTH, ncol // tn),
        in_specs=[
            pl.BlockSpec((rows, D_MODEL), lambda l, j: (0, 0)),
            pl.BlockSpec((None, D_MODEL, tn), lambda l, j: (l, 0, j)),
            pl.BlockSpec((None, 1, tn), lambda l, j: (l, 0, j)),
        ],
        out_specs=pl.BlockSpec((None, rows, tn), lambda l, j: (l, 0, j)),
        out_shape=jax.ShapeDtypeStruct((DEPTH, rows, ncol), F32),
        compiler_params=_cparams(("arbitrary", "arbitrary")),
        name="ada",
    )(c_all, w_ada, b_ada.reshape(DEPTH, 1, ncol))


def _norm_rows(x, g, eps):
    return x * lax.rsqrt(jnp.mean(x * x, axis=-1, keepdims=True) + eps) * g


def _normmm_kernel(x_ref, g_ref, sc_ref, sh_ref, w_ref, o_ref, h_scr):
    @pl.when(pl.program_id(1) == 0)
    def _():
        h = _norm_rows(x_ref[...], g_ref[...], 1e-6) * (1.0 + sc_ref[...]) + sh_ref[...]
        h_scr[...] = h.astype(BF16)

    o_ref[...] = _dot(h_scr[...], w_ref[...]).astype(o_ref.dtype)


def _normmm(x, g, mod, sc_blk, sh_blk, w, tm, tn, out_dtype=F32):
    n, d = x.shape
    ncol = w.shape[1]
    nseq, r, _ = mod.shape
    tiles_per_seq = (n // tm) // nseq
    return pl.pallas_call(
        _normmm_kernel,
        grid=(n // tm, ncol // tn),
        in_specs=[
            pl.BlockSpec((tm, d), lambda i, j: (i, 0)),
            pl.BlockSpec((1, d), lambda i, j: (0, 0)),
            pl.BlockSpec((None, r, d), lambda i, j: (i // tiles_per_seq, 0, sc_blk)),
            pl.BlockSpec((None, r, d), lambda i, j: (i // tiles_per_seq, 0, sh_blk)),
            pl.BlockSpec((d, tn), lambda i, j: (0, j)),
        ],
        out_specs=pl.BlockSpec((tm, tn), lambda i, j: (i, j)),
        out_shape=jax.ShapeDtypeStruct((n, ncol), out_dtype),
        scratch_shapes=[pltpu.VMEM((tm, d), BF16)],
        compiler_params=_cparams(("arbitrary", "arbitrary")),
        name="normmm",
    )(x, g.reshape(1, d), mod, mod, w)


def _normT_kernel(x_ref, g_ref, sc_ref, sh_ref, o_ref):
    h = _norm_rows(x_ref[...], g_ref[...], 1e-6) * (1.0 + sc_ref[...]) + sh_ref[...]
    o_ref[...] = h.T.astype(BF16)


def _normT(x, g, mod, sc_blk, sh_blk, tm):
    n, d = x.shape
    nseq, r, _ = mod.shape
    tiles_per_seq = (n // tm) // nseq
    return pl.pallas_call(
        _normT_kernel,
        grid=(n // tm,),
        in_specs=[
            pl.BlockSpec((tm, d), lambda i: (i, 0)),
            pl.BlockSpec((1, d), lambda i: (0, 0)),
            pl.BlockSpec((None, r, d), lambda i: (i // tiles_per_seq, 0, sc_blk)),
            pl.BlockSpec((None, r, d), lambda i: (i // tiles_per_seq, 0, sh_blk)),
        ],
        out_specs=pl.BlockSpec((d, tm), lambda i: (0, i)),
        out_shape=jax.ShapeDtypeStruct((d, n), BF16),
        compiler_params=_cparams(("arbitrary",)),
        name="normT",
    )(x, g.reshape(1, d), mod, mod)


def _final_norm_kernel(x_ref, g_ref, o_ref):
    o_ref[...] = _norm_rows(x_ref[...], g_ref[...], 1e-6)


def _final_norm(x, g, tm):
    n, d = x.shape
    return pl.pallas_call(
        _final_norm_kernel,
        grid=(n // tm,),
        in_specs=[pl.BlockSpec((tm, d), lambda i: (i, 0)), pl.BlockSpec((1, d), lambda i: (0, 0))],
        out_specs=pl.BlockSpec((tm, d), lambda i: (i, 0)),
        out_shape=jax.ShapeDtypeStruct((n, d), F32),
        compiler_params=_cparams(("arbitrary",)),
        name="final_norm",
    )(x, g.reshape(1, d))


def _merge_kernel(o0_ref, o1_ref, o2_ref, g0_ref, g1_ref, g2_ref, w_ref, out_ref):
    acc = jax.nn.sigmoid(g0_ref[...]) * _dot(o0_ref[...], w_ref[0])
    acc += jax.nn.sigmoid(g1_ref[...]) * _dot(o1_ref[...], w_ref[1])
    acc += jax.nn.sigmoid(g2_ref[...]) * _dot(o2_ref[...], w_ref[2])
    out_ref[...] = acc.astype(out_ref.dtype)


def _merge(o_att, o_rwkv, o_ssm, gates, wb, tm, tn):
    n = o_att.shape[0]
    gblk = D_MODEL // tn
    o_spec = pl.BlockSpec((tm, BRANCH_WIDTH), lambda i, j: (i, 0))
    g_specs = [pl.BlockSpec((tm, tn), functools.partial(lambda i, j, b: (i, b * gblk + j), b=b)) for b in range(3)]
    return pl.pallas_call(
        _merge_kernel,
        grid=(n // tm, D_MODEL // tn),
        in_specs=[o_spec, o_spec, o_spec, *g_specs,
                  pl.BlockSpec((N_BRANCHES, BRANCH_WIDTH, tn), lambda i, j: (0, 0, j))],
        out_specs=pl.BlockSpec((tm, tn), lambda i, j: (i, j)),
        out_shape=jax.ShapeDtypeStruct((n, D_MODEL), BF16),
        compiler_params=_cparams(("arbitrary", "arbitrary")),
        name="merge",
    )(o_att, o_rwkv, o_ssm, gates, gates, gates, wb)


def _proj_res_kernel(a_ref, w_ref, x_ref, g_ref, o_ref):
    o_ref[...] = x_ref[...] + g_ref[...] * _dot(a_ref[...], w_ref[...])


def _proj_res(a, w, x, mod, g_blk, tm, tn):
    n, k = a.shape
    nseq, r, _ = mod.shape
    tiles_per_seq = (n // tm) // nseq
    nblk = D_MODEL // tn
    return pl.pallas_call(
        _proj_res_kernel,
        grid=(n // tm, D_MODEL // tn),
        in_specs=[
            pl.BlockSpec((tm, k), lambda i, j: (i, 0)),
            pl.BlockSpec((k, tn), lambda i, j: (0, j)),
            pl.BlockSpec((tm, tn), lambda i, j: (i, j)),
            pl.BlockSpec((None, r, tn), lambda i, j: (i // tiles_per_seq, 0, g_blk * nblk + j)),
        ],
        out_specs=pl.BlockSpec((tm, tn), lambda i, j: (i, j)),
        out_shape=jax.ShapeDtypeStruct((n, D_MODEL), F32),
        compiler_params=_cparams(("arbitrary", "arbitrary")),
        name="proj_res",
    )(a, w, x, mod)


def _rel_bucket(dist):
    n = jnp.maximum(dist, 0)
    max_exact = REL_BUCKETS // 2
    nf = jnp.maximum(n, 1).astype(F32)
    large = max_exact + (jnp.log(nf / max_exact) / math.log(REL_MAX_DIST / max_exact)
                         * (REL_BUCKETS - max_exact)).astype(jnp.int32)
    large = jnp.minimum(large, REL_BUCKETS - 1)
    return jnp.where(n < max_exact, n, large)


def _bias_of_dist(rel_bias, dist):
    b = jnp.moveaxis(rel_bias[_rel_bucket(dist)].astype(F32), -1, 0)
    return jnp.where(dist[None] >= 0, b, NEG_BIG)


def _softmax_step(s, v_bf16, m_scr, l_scr, acc_scr):
    m_prev = m_scr[...]
    m_new = jnp.maximum(m_prev, jnp.max(s, axis=-1, keepdims=True))
    alpha = jnp.exp(m_prev - m_new)
    p = jnp.exp(s - m_new)
    l_scr[...] = alpha * l_scr[...] + jnp.sum(p, axis=-1, keepdims=True)
    acc_scr[...] = alpha * acc_scr[...] + _dot(p.astype(BF16), v_bf16)
    m_scr[...] = m_new


def _two_map_queries(q, scale):
    lane = lax.broadcasted_iota(jnp.int32, q.shape, 1)
    qs = q * scale
    q0 = jnp.where(lane < ATT_DH, qs, 0.0)
    q1 = jnp.where(lane >= ATT_DH, qs, 0.0)
    return jnp.concatenate([q0, q1], axis=0).astype(BF16)


def _diff_finalize(acc, l, lam, g, rows, post_scale):
    o0 = acc[:rows] / l[:rows]
    o1 = acc[rows:] / l[rows:]
    o = o0 - lam * o1
    o = o * lax.rsqrt(jnp.mean(o * o, axis=-1, keepdims=True) + 1e-5) * g
    return o * post_scale


def _attn_prompt_kernel(qi_ref, ki_ref, lam_ref, q_ref, k_ref, v_ref, bias_ref, g_ref, o_ref,
                        m_scr, l_scr, acc_scr, *, tq, post_scale):
    p = pl.program_id(2)
    qi = qi_ref[p]
    ki = ki_ref[p]

    @pl.when(ki == 0)
    def _():
        m_scr[...] = jnp.full(m_scr.shape, NEG_BIG, F32)
        l_scr[...] = jnp.zeros(l_scr.shape, F32)
        acc_scr[...] = jnp.zeros(acc_scr.shape, F32)

    qq = _two_map_queries(q_ref[...], ATT_DH ** -0.5)
    s = _dot_nt(qq, k_ref[...].astype(BF16))
    b = bias_ref[...]
    s = s + jnp.concatenate([b, b], axis=0)
    _softmax_step(s, v_ref[...].astype(BF16), m_scr, l_scr, acc_scr)

    @pl.when(ki == qi)
    def _():
        o = _diff_finalize(acc_scr[...], l_scr[...], lam_ref[0, 0], g_ref[...], tq, post_scale)
        o_ref[...] = o.astype(o_ref.dtype)


def _attn_prompt(qkv, rel_bias, lam, subln, lam_init, bsz, t, tq):
    nq = t // tq
    pairs = [(a, b) for a in range(nq) for b in range(a + 1)]
    qi_tab = jnp.asarray([a for a, _ in pairs], jnp.int32)
    ki_tab = jnp.asarray([b for _, b in pairs], jnp.int32)
    assert tq >= REL_MAX_DIST
    ii = jnp.arange(tq, dtype=jnp.int32)
    d0 = ii[:, None] - ii[None, :]
    tiles = jnp.stack([_bias_of_dist(rel_bias, d0), _bias_of_dist(rel_bias, d0 + tq),
                       _bias_of_dist(rel_bias, d0 + 2 * tq)], axis=1)
    qkv3 = qkv.reshape(bsz, t, 3 * ATT_QK)
    kern = functools.partial(_attn_prompt_kernel, tq=tq, post_scale=1.0 - lam_init)
    grid_spec = pltpu.PrefetchScalarGridSpec(
        num_scalar_prefetch=2,
        grid=(bsz, ATT_HEADS, len(pairs)),
        in_specs=[
            pl.BlockSpec(memory_space=pltpu.SMEM),
            pl.BlockSpec((None, tq, LANES), lambda b, h, p, qt, kt: (b, qt[p], h)),
            pl.BlockSpec((None, tq, LANES), lambda b, h, p, qt, kt: (b, kt[p], ATT_HEADS + h)),
            pl.BlockSpec((None, tq, LANES), lambda b, h, p, qt, kt: (b, kt[p], 2 * ATT_HEADS + h)),
            pl.BlockSpec((None, None, tq, tq), lambda b, h, p, qt, kt: (h, jnp.minimum(qt[p] - kt[p], 2), 0, 0)),
            pl.BlockSpec((1, LANES), lambda b, h, p, qt, kt: (0, 0)),
        ],
        out_specs=pl.BlockSpec((None, tq, LANES), lambda b, h, p, qt, kt: (b, qt[p], h)),
        scratch_shapes=[pltpu.VMEM((2 * tq, 1), F32), pltpu.VMEM((2 * tq, 1), F32),
                        pltpu.VMEM((2 * tq, LANES), F32)],
    )
    out = pl.pallas_call(
        kern,
        grid_spec=grid_spec,
        out_shape=jax.ShapeDtypeStruct((bsz, t, BRANCH_WIDTH), BF16),
        compiler_params=_cparams(("arbitrary", "arbitrary", "arbitrary")),
        name="attn_prompt",
    )(qi_tab, ki_tab, lam.reshape(1, 1), qkv3, qkv3, qkv3, tiles, subln.reshape(1, ATT_VD))
    return out.reshape(bsz * t, BRANCH_WIDTH)


def _attn_sample_kernel(pt_ref, lam_ref, q_ref, kn_ref, vn_ref, bias_ref, biasn_ref, g_ref, *rest,
                        pps, rows, post_scale):
    kp_refs = rest[:pps]
    vp_refs = rest[pps:2 * pps]
    o_ref, m_scr, l_scr, acc_scr = rest[2 * pps:]
    s_idx = pl.program_id(1)
    last = s_idx == pl.num_programs(1) - 1

    @pl.when(s_idx == 0)
    def _():
        m_scr[...] = jnp.full(m_scr.shape, NEG_BIG, F32)
        l_scr[...] = jnp.zeros(l_scr.shape, F32)
        acc_scr[...] = jnp.zeros(acc_scr.shape, F32)

    qq = _two_map_queries(q_ref[...], ATT_DH ** -0.5)
    for r in range(pps):
        kb = kp_refs[r][...].reshape(PAGE_SIZE * ATT_HEADS, LANES).astype(BF16)
        vb = vp_refs[r][...].reshape(PAGE_SIZE * ATT_HEADS, LANES).astype(BF16)
        s = _dot_nt(qq, kb)
        if r == pps - 1:
            b = jnp.where(last, bias_ref[1], bias_ref[0])
        else:
            b = bias_ref[0]
        s = s + jnp.concatenate([b, b], axis=0)
        _softmax_step(s, vb, m_scr, l_scr, acc_scr)

    @pl.when(last)
    def _():
        s = _dot_nt(qq, kn_ref[...].astype(BF16))
        b = biasn_ref[...]
        s = s + jnp.concatenate([b, b], axis=0)
        _softmax_step(s, vn_ref[...].astype(BF16), m_scr, l_scr, acc_scr)
        o = _diff_finalize(acc_scr[...], l_scr[...], lam_ref[0, 0], g_ref[...], rows, post_scale)
        o_ref[...] = o.astype(o_ref.dtype)


def _attn_sample(qkv, cache_k, cache_v, page_table, layer, rel_bias, lam, subln, lam_init, bsz, t):
    n_pages = page_table.shape[1]
    past = n_pages * PAGE_SIZE
    pps = 4
    assert n_pages % pps == 0
    rows = t * ATT_HEADS
    q = qkv[:, :ATT_QK].reshape(bsz, rows, LANES)
    kn = qkv[:, ATT_QK:2 * ATT_QK].reshape(bsz, rows, LANES)
    vn = qkv[:, 2 * ATT_QK:].reshape(bsz, rows, LANES)
    tok = jnp.repeat(jnp.arange(t, dtype=jnp.int32), ATT_HEADS)
    hq = jnp.tile(jnp.arange(ATT_HEADS, dtype=jnp.int32), t)
    kk = jnp.repeat(jnp.arange(PAGE_SIZE, dtype=jnp.int32), ATT_HEADS)
    hk = jnp.tile(jnp.arange(ATT_HEADS, dtype=jnp.int32), PAGE_SIZE)

    def tile(dist, hk_):
        b = rel_bias[_rel_bucket(dist), hq[:, None]].astype(F32)
        return jnp.where((dist >= 0) & (hq[:, None] == hk_[None, :]), b, NEG_BIG)

    assert PAGE_SIZE >= REL_MAX_DIST
    far = tile(jnp.broadcast_to(jnp.int32(2 * PAGE_SIZE), (rows, PAGE_SIZE * ATT_HEADS)) + tok[:, None] - kk[None, :], hk)
    near = tile(PAGE_SIZE + tok[:, None] - kk[None, :], hk)
    bias_pages = jnp.stack([far, near])
    bias_new = tile(tok[:, None] - tok[None, :], hq)
    kern = functools.partial(_attn_sample_kernel, pps=pps, rows=rows, post_scale=1.0 - lam_init)
    page_block = (None, None, PAGE_SIZE, ATT_HEADS, LANES)
    page_specs = [pl.BlockSpec(page_block, functools.partial(lambda b, s, pt, r: (pt[b, s * pps + r], layer, 0, 0, 0), r=r))
                  for r in range(pps)]
    grid_spec = pltpu.PrefetchScalarGridSpec(
        num_scalar_prefetch=1,
        grid=(bsz, n_pages // pps),
        in_specs=[
            pl.BlockSpec(memory_space=pltpu.SMEM),
            pl.BlockSpec((None, rows, LANES), lambda b, s, pt: (b, 0, 0)),
            pl.BlockSpec((None, rows, LANES), lambda b, s, pt: (b, 0, 0)),
            pl.BlockSpec((None, rows, LANES), lambda b, s, pt: (b, 0, 0)),
            pl.BlockSpec((2, rows, PAGE_SIZE * ATT_HEADS), lambda b, s, pt: (0, 0, 0)),
            pl.BlockSpec((rows, rows), lambda b, s, pt: (0, 0)),
            pl.BlockSpec((1, LANES), lambda b, s, pt: (0, 0)),
            *page_specs, *page_specs,
        ],
        out_specs=pl.BlockSpec((None, rows, LANES), lambda b, s, pt: (b, 0, 0)),
        scratch_shapes=[pltpu.VMEM((2 * rows, 1), F32), pltpu.VMEM((2 * rows, 1), F32),
                        pltpu.VMEM((2 * rows, LANES), F32)],
    )
    out = pl.pallas_call(
        kern,
        grid_spec=grid_spec,
        out_shape=jax.ShapeDtypeStruct((bsz, rows, LANES), BF16),
        compiler_params=_cparams(("arbitrary", "arbitrary")),
        name="attn_sample",
    )(page_table, lam.reshape(1, 1), q, kn, vn, bias_pages, bias_new, subln.reshape(1, ATT_VD),
      *([cache_k] * pps), *([cache_v] * pps))
    return out.reshape(bsz * t, BRANCH_WIDTH)


RW_LORA_OFF = 3 * BRANCH_WIDTH
RW_G_OFF = RW_LORA_OFF + LANES


def _rwkv_prep_kernel(x_ref, prev_ref, shift_ref, mu_ref, w0_ref, w2_ref, a0_ref, a2_ref, g2_ref,
                      kkw_ref, ka_ref, rk_ref, e_ref,
                      a_out, wr_out, w_out, b_out, k_out, br_out, kr_out, v_out, bon_out, g_out):
    bw = BRANCH_WIDTH
    x = x_ref[...]
    tm = x.shape[0]
    first = pl.program_id(1) == 0
    prev_row = jnp.where(first, shift_ref[...], prev_ref[SUBLANES - 1:SUBLANES, :])
    row = lax.broadcasted_iota(jnp.int32, x.shape, 0)
    prev = jnp.where(row == 0, prev_row, pltpu.roll(x, 1, axis=0))
    pm = x + (prev - x) * mu_ref[...]
    r = pm[:, :bw]
    k = pm[:, bw:2 * bw]
    v = pm[:, 2 * bw:3 * bw]
    lora = pm[:, RW_LORA_OFF:RW_LORA_OFF + LANES]
    glo = pm[:, RW_G_OFF:RW_G_OFF + 2 * LANES]
    w = -_softplus(-(w0_ref[...] + _dot(jnp.tanh(lora).astype(BF16), w2_ref[...]))) - 0.5
    decay = jnp.exp(-jnp.exp(w))
    a = jax.nn.sigmoid(a0_ref[...] + _dot(lora.astype(BF16), a2_ref[...]))
    g = _dot(jax.nn.sigmoid(glo).astype(BF16), g2_ref[...])
    kk = k * kkw_ref[...]
    n2 = _segsum_bcast(kk * kk, e_ref, LANES)
    kk = kk / jnp.maximum(jnp.sqrt(n2), 1e-12)
    k2 = k * (1.0 + (a - 1.0) * ka_ref[...])
    bm = kk * a
    a_out[...] = -kk
    wr_out[...] = decay * r
    w_out[...] = decay
    b_out[...] = bm
    k_out[...] = k2
    br_out[...] = _segsum_bcast(bm * r, e_ref, LANES)
    kr_out[...] = _segsum_bcast(k2 * r, e_ref, LANES)
    v_out[...] = v
    bon_out[...] = _segsum_bcast(r * k2 * rk_ref[...], e_ref, LANES) * v
    g_out[...] = g


def _rwkv_prep(rw, shift0, P, bsz, t, tm):
    n = rw.shape[0]
    bw = BRANCH_WIDTH
    cp = RWKV_COLS_PAD
    tiles = t // tm
    vec = lambda nm: pl.BlockSpec((1, bw), lambda b, i: (0, 0))
    out_spec = pl.BlockSpec((tm, bw), lambda b, i: (b * tiles + i, 0))
    rows8 = tm // SUBLANES
    outs = pl.pallas_call(
        _rwkv_prep_kernel,
        grid=(bsz, tiles),
        in_specs=[
            pl.BlockSpec((tm, cp), lambda b, i: (b * tiles + i, 0)),
            pl.BlockSpec((SUBLANES, cp), lambda b, i: (jnp.maximum((b * tiles + i) * rows8 - 1, 0), 0)),
            pl.BlockSpec((None, 1, cp), lambda b, i: (b, 0, 0)),
            pl.BlockSpec((1, cp), lambda b, i: (0, 0)),
            vec("w0"),
            pl.BlockSpec((LANES, bw), lambda b, i: (0, 0)),
            vec("a0"),
            pl.BlockSpec((LANES, bw), lambda b, i: (0, 0)),
            pl.BlockSpec((2 * LANES, bw), lambda b, i: (0, 0)),
            vec("kk"), vec("ka"), vec("rk"),
            pl.BlockSpec((LANES, LANES), lambda b, i: (0, 0)),
        ],
        out_specs=[out_spec] * 10,
        out_shape=[jax.ShapeDtypeStruct((n, bw), F32)] * 10,
        compiler_params=_cparams(("arbitrary", "arbitrary")),
        name="rwkv_prep",
    )(rw, rw, shift0.reshape(bsz, 1, cp), P["mu"], P["w0"], P["w2"], P["a0"], P["a2"], P["g2"],
      P["kk"], P["ka"], P["rk"], _block_ones(LANES, RWKV_DH))
    return outs


def _rwkv_scan_kernel(a_ref, wr_ref, w_ref, b_ref, k_ref, br_ref, kr_ref, v_ref, bon_ref, g_ref,
                      s0_ref, lng_ref, lnb_ref, e_ref, o_ref, sT_ref, s_scr, y_scr, *, bb, tc, pg):
    c = pl.program_id(2)

    @pl.when(c == 0)
    def _():
        s_scr[...] = s0_ref[...]

    e = e_ref[...]
    sub = lax.broadcasted_iota(jnp.int32, (RWKV_DH, LANES), 0)
    lane = lax.broadcasted_iota(jnp.int32, (RWKV_DH, LANES), 1)
    diag = (lane % RWKV_DH) == sub

    def group(tg, carry):
        t0 = pl.multiple_of(tg * SUBLANES, SUBLANES)
        for b in range(bb):
            for p in range(pg):
                cs = pl.ds(p * LANES, LANES)
                ld = lambda ref: ref[b, pl.ds(t0, SUBLANES), cs]
                av, wrv, wv, bv, kv, brv, krv, vv = (ld(r_) for r_ in (
                    a_ref, wr_ref, w_ref, b_ref, k_ref, br_ref, kr_ref, v_ref))
                s = s_scr[b, p]
                ys = []
                for r in range(SUBLANES):
                    row = lambda x: x[r:r + 1, :]
                    p1 = s * row(av)
                    p2 = s * row(wrv)
                    z = jnp.where(diag, row(vv), 0.0)
                    p1h, p1l = _split2(p1)
                    lhs = jnp.concatenate([p1h, p1l, p2.astype(BF16), z.astype(BF16)], axis=0)
                    res = _dot(lhs, e)
                    sa = res[0:RWKV_DH] + res[RWKV_DH:2 * RWKV_DH]
                    m2 = res[2 * RWKV_DH:3 * RWKV_DH]
                    vb = res[3 * RWKV_DH:4 * RWKV_DH]
                    s = s * row(wv) + sa * row(bv) + vb * row(kv)
                    yb = m2 + sa * row(brv) + vb * row(krv)
                    ys.append(jnp.sum(jnp.where(diag, yb, 0.0), axis=0, keepdims=True))
                s_scr[b, p] = s
                y_scr[b, pl.ds(t0, SUBLANES), cs] = jnp.concatenate(ys, axis=0)
        return carry

    lax.fori_loop(0, tc // SUBLANES, group, 0)

    for b in range(bb):
        y = y_scr[b]
        mu = _segsum_bcast(y, e_ref, LANES) * (1.0 / RWKV_DH)
        d = y - mu
        var = _segsum_bcast(d * d, e_ref, LANES) * (1.0 / RWKV_DH)
        yn = d * lax.rsqrt(var + RWKV_LN_EPS) * lng_ref[...] + lnb_ref[...]
        o_ref[b] = ((yn + bon_ref[b]) * g_ref[b]).astype(o_ref.dtype)

    @pl.when(c == pl.num_programs(2) - 1)
    def _():
        sT_ref[...] = s_scr[...]


def _rwkv_scan(prep, s0, lng, lnb, bsz, t, bb, tc, pg):
    bw = BRANCH_WIDTH
    npair = RWKV_HEADS // 2
    assert tc % SUBLANES == 0 and npair % pg == 0
    wcols = pg * LANES
    seq = [z.reshape(bsz, t, bw) for z in prep]
    blk = pl.BlockSpec((bb, tc, wcols), lambda b, q, c: (b, c, q))
    st = pl.BlockSpec((bb, pg, RWKV_DH, LANES), lambda b, q, c: (b, q, 0, 0))
    vec = pl.BlockSpec((1, wcols), lambda b, q, c: (0, q))
    kern = functools.partial(_rwkv_scan_kernel, bb=bb, tc=tc, pg=pg)
    o, s_t = pl.pallas_call(
        kern,
        grid=(bsz // bb, npair // pg, t // tc),
        in_specs=[blk] * 10 + [st, vec, vec, pl.BlockSpec((LANES, LANES), lambda b, q, c: (0, 0))],
        out_specs=[blk, st],
        out_shape=[jax.ShapeDtypeStruct((bsz, t, bw), BF16),
                   jax.ShapeDtypeStruct((bsz, npair, RWKV_DH, LANES), F32)],
        scratch_shapes=[pltpu.VMEM((bb, pg, RWKV_DH, LANES), F32), pltpu.VMEM((bb, tc, wcols), F32)],
        compiler_params=_cparams(("arbitrary", "arbitrary", "arbitrary")),
        name="rwkv_scan",
    )(*seq, s0, lng, lnb, _block_ones(LANES, RWKV_DH))
    return o.reshape(bsz * t, bw), s_t


def _mamba_kernel(x_ref, conv0_ref, h0_ref, cw_ref, cb_ref, dtb_ref, alog_ref, dvec_ref, nrm_ref,
                  tri_ref, eye_ref, xp_ref, xpt_ref, e256_ref, o_ref, hT_ref, tail_scr, h_scr, *, ln):
    bw = BRANCH_WIDTH
    gn = SSM_GROUPS * SSM_STATE
    npair = SSM_HEADS // 2
    c = pl.program_id(1)

    @pl.when(c == 0)
    def _():
        tail_scr[...] = conv0_ref[...]
        h_scr[...] = h0_ref[...]

    blk = x_ref[...]
    z = blk[:, :bw]
    xbc = blk[:, bw:bw + SSM_CONV_DIM]
    dtr = blk[:, bw + SSM_CONV_DIM:bw + SSM_CONV_DIM + LANES]
    xpad = jnp.concatenate([tail_scr[...], xbc], axis=0)
    conv = cb_ref[...]
    for i in range(SSM_CONV):
        off = SUBLANES - (SSM_CONV - 1) + i
        conv = conv + xpad[off:off + ln] * cw_ref[i:i + 1, :]
    tail_scr[...] = xpad[ln:ln + SUBLANES]
    conv = _silu(conv)
    xs = conv[:, :bw]
    lane = lax.broadcasted_iota(jnp.int32, (ln, LANES), 1)
    dt = jnp.where(lane < SSM_HEADS, _softplus(dtr + dtb_ref[...]), 0.0)
    a_neg = -jnp.exp(alog_ref[...])
    a = dt * a_neg
    ah, am_, al = _split3(a)
    tri = tri_ref[...]
    a_cs = _dot(tri, ah) + _dot(tri, am_) + _dot(tri, al)
    ch, cm, cl = _split3(a_cs)
    eye = eye_ref[...]
    a_cs_t = _dot_nt(eye, ch) + _dot_nt(eye, cm) + _dot_nt(eye, cl)
    a_tot = a_cs[ln - 1:ln, :]
    xp = xp_ref[...]
    dt_x = _dot_x3(dt, xp)
    acs_x = _dot_x3(a_cs, xp)
    atot_x = _dot_x3(jnp.broadcast_to(a_tot, (SUBLANES, LANES)), xp)[0:1]
    th, tm_, tl = _split3(jnp.broadcast_to(a_tot, (SUBLANES, LANES)))
    xpt = xpt_ref[...]
    atot_col = (_dot_nt(xpt, th) + _dot_nt(xpt, tm_) + _dot_nt(xpt, tl))[:, 0:1]
    xc = xs * dt_x
    xd = xc * jnp.exp(atot_x - acs_x)
    row_i = lax.broadcasted_iota(jnp.int32, (ln, ln), 0)
    col_i = lax.broadcasted_iota(jnp.int32, (ln, ln), 1)
    tril = row_i >= col_i
    lane_h = lax.broadcasted_iota(jnp.int32, (ln, LANES), 1) < SSM_HEAD_DIM
    y_parts = []
    for g in range(SSM_GROUPS):
        bg = conv[:, bw + g * SSM_STATE:bw + (g + 1) * SSM_STATE].astype(BF16)
        cg = conv[:, bw + gn + g * SSM_STATE:bw + gn + (g + 1) * SSM_STATE].astype(BF16)
        cbm = _dot_nt(cg, bg)
        for pr in range(2):
            pi = g * 2 + pr
            cs = slice(pi * LANES, (pi + 1) * LANES)
            xcp = xc[:, cs]
            y = None
            for e in range(2):
                hd = 2 * pi + e
                seg = a_cs[:, hd:hd + 1] - a_cs_t[hd:hd + 1, :]
                wm = jnp.where(tril, cbm * jnp.exp(jnp.where(tril, seg, 0.0)), 0.0).astype(BF16)
                xm = jnp.where(lane_h if e == 0 else jnp.logical_not(lane_h), xcp, 0.0).astype(BF16)
                part = _dot(wm, xm)
                y = part if y is None else y + part
            hp = h_scr[pi]
            y = y + _dot_nt(cg, hp.astype(BF16)) * jnp.exp(acs_x[:, cs])
            y_parts.append(y)
            h_new = jnp.exp(atot_col[pi * LANES:(pi + 1) * LANES, :]) * hp + _dot_tn(xd[:, cs].astype(BF16), bg)
            h_scr[pi] = h_new
    y = jnp.concatenate(y_parts, axis=1)
    y = (y + dvec_ref[...] * xs) * _silu(z)
    gw = bw // SSM_GROUPS
    ms = _segsum_bcast(y * y, e256_ref, gw) * (1.0 / gw)
    y = y * lax.rsqrt(ms + 1e-5) * nrm_ref[...]
    o_ref[...] = y.astype(o_ref.dtype)

    @pl.when(c == pl.num_programs(1) - 1)
    def _():
        hT_ref[...] = h_scr[...]


def _mamba(sm, conv0, h0, P, bsz, t):
    bw = BRANCH_WIDTH
    ln = min(SSM_CHUNK, t)
    assert t % ln == 0
    nc = t // ln
    npair = SSM_HEADS // 2
    tri = jnp.asarray(np.tril(np.ones((ln, ln), np.float32)), dtype=BF16)
    eye = jnp.asarray(np.eye(LANES, dtype=np.float32), dtype=BF16)
    xp_np = (np.arange(LANES)[:, None] == (np.arange(bw)[None, :] // SSM_HEAD_DIM)).astype(np.float32)
    xp = jnp.asarray(xp_np, dtype=BF16)
    xpt = jnp.asarray(xp_np.T.copy(), dtype=BF16)
    const = lambda shape: pl.BlockSpec(shape, lambda b, c: tuple(0 for _ in shape))
    kern = functools.partial(_mamba_kernel, ln=ln)
    o, h_t = pl.pallas_call(
        kern,
        grid=(bsz, nc),
        in_specs=[
            pl.BlockSpec((ln, SSM_COLS_PAD), lambda b, c: (b * nc + c, 0)),
            pl.BlockSpec((None, SUBLANES, SSM_CONV_DIM), lambda b, c: (b, 0, 0)),
            pl.BlockSpec((None, npair, LANES, SSM_STATE), lambda b, c: (b, 0, 0, 0)),
            const((SSM_CONV, SSM_CONV_DIM)), const((1, SSM_CONV_DIM)), const((1, LANES)), const((1, LANES)),
            const((1, bw)), const((1, bw)), const((ln, ln)), const((LANES, LANES)),
            const((LANES, bw)), const((bw, LANES)), const((bw // SSM_GROUPS, bw // SSM_GROUPS)),
        ],
        out_specs=[pl.BlockSpec((ln, bw), lambda b, c: (b * nc + c, 0)),
                   pl.BlockSpec((None, npair, LANES, SSM_STATE), lambda b, c: (b, 0, 0, 0))],
        out_shape=[jax.ShapeDtypeStruct((bsz * t, bw), BF16),
                   jax.ShapeDtypeStruct((bsz, npair, LANES, SSM_STATE), F32)],
        scratch_shapes=[pltpu.VMEM((SUBLANES, SSM_CONV_DIM), F32), pltpu.VMEM((npair, LANES, SSM_STATE), F32)],
        compiler_params=_cparams(("arbitrary", "arbitrary")),
        name="mamba",
    )(sm, conv0, h0, P["cw"], P["cb"], P["dtb"], P["alog"], P["dvec"], P["nrm"], tri, eye, xp, xpt,
      _block_ones(bw // SSM_GROUPS, bw // SSM_GROUPS))
    return o, h_t


def _topk_rows(s, k):
    rows = s.shape[0]
    iota = lax.broadcasted_iota(jnp.int32, s.shape, 0)
    outs = []
    for _ in range(k):
        m = jnp.max(s, axis=0, keepdims=True)
        outs.append(m)
        idx = jnp.min(jnp.where(s == m, iota, rows), axis=0, keepdims=True)
        s = jnp.where(iota == idx, -jnp.inf, s)
    return jnp.concatenate(outs, axis=0)


def _peer_route_kernel(q_ref, k1_ref, k2_ref, s1_ref, s2_ref, e1_ref, e2_ref, tau_ref):
    vals = []
    sc = []
    for half, (k_ref, s_ref) in enumerate(((k1_ref, s1_ref), (k2_ref, s2_ref))):
        kh, kl = _split2(k_ref[...])
        qh, ql = _split2(q_ref[:, half * PEER_HALF:(half + 1) * PEER_HALF])
        s = _dot_nt(kh, qh) + _dot_nt(kh, ql) + _dot_nt(kl, qh)
        s_ref[...] = s
        sc.append(s)
        vals.append(_topk_rows(s, PEER_TOPK))
    v1, v2 = vals
    cand = jnp.concatenate([v1[i:i + 1, :] + v2 for i in range(PEER_TOPK)], axis=0)
    top = _topk_rows(cand, PEER_TOPK)
    tau = top[PEER_TOPK - 1:PEER_TOPK, :]
    zsum = jnp.sum(jnp.exp(top - top[0:1, :]), axis=0, keepdims=True)
    e1_ref[...] = jnp.exp(sc[0] - v1[0:1, :]) / zsum
    e2_ref[...] = jnp.exp(sc[1] - v2[0:1, :])
    tau_ref[...] = jnp.broadcast_to(tau, tau_ref.shape)


def _peer_route(q, k1, k2, tt):
    n = q.shape[0]
    arr = lambda: jax.ShapeDtypeStruct((PEER_HEADS, PEER_NKEYS, n), F32)
    spec = pl.BlockSpec((None, PEER_NKEYS, tt), lambda i, h: (h, 0, i))
    return pl.pallas_call(
        _peer_route_kernel,
        grid=(n // tt, PEER_HEADS),
        in_specs=[pl.BlockSpec((tt, PEER_KEY_DIM), lambda i, h: (i, h)),
                  pl.BlockSpec((PEER_NKEYS, PEER_HALF), lambda i, h: (0, 0)),
                  pl.BlockSpec((PEER_NKEYS, PEER_HALF), lambda i, h: (0, 0))],
        out_specs=[spec, spec, spec, spec, pl.BlockSpec((None, SUBLANES, tt), lambda i, h: (h, 0, i))],
        out_shape=[arr(), arr(), arr(), arr(), jax.ShapeDtypeStruct((PEER_HEADS, SUBLANES, n), F32)],
        compiler_params=_cparams(("arbitrary", "arbitrary")),
        name="peer_route",
    )(q, k1, k2)


def _peer_dense_kernel(ht_ref, u_ref, vt_ref, s1_ref, s2_ref, e1_ref, e2_ref, tau_ref, o_ref, ca_scr, *, ib):
    j = pl.program_id(1)

    @pl.when(j == 0)
    def _():
        o_ref[...] = jnp.zeros(o_ref.shape, F32)

    ht = ht_ref[...]
    for ii in range(ib):
        act = _dot(u_ref[ii * PEER_NKEYS:(ii + 1) * PEER_NKEYS, :], ht)
        act = 0.5 * act * (1.0 + lax.erf(act * (2.0 ** -0.5)))
        coef = jnp.zeros(act.shape, F32)
        for h in range(PEER_HEADS):
            s1 = s1_ref[h, ii:ii + 1, :]
            keep = (s1 + s2_ref[h]) >= tau_ref[h, 0:1, :]
            coef = coef + jnp.where(keep, e1_ref[h, ii:ii + 1, :] * e2_ref[h], 0.0)
        ca_scr[ii * PEER_NKEYS:(ii + 1) * PEER_NKEYS, :] = (coef * act).astype(BF16)
    o_ref[...] += _dot(vt_ref[...], ca_scr[...])


def _peer_dense(ht, u, vt, s1, s2, e1, e2, tau, tt, ib):
    d, n = ht.shape
    eb = ib * PEER_NKEYS
    kern = functools.partial(_peer_dense_kernel, ib=ib)
    full = pl.BlockSpec((PEER_HEADS, PEER_NKEYS, tt), lambda i, j: (0, 0, i))
    part = pl.BlockSpec((PEER_HEADS, ib, tt), lambda i, j: (0, j, i))
    return pl.pallas_call(
        kern,
        grid=(n // tt, PEER_EXPERTS // eb),
        in_specs=[
            pl.BlockSpec((d, tt), lambda i, j: (0, i)),
            pl.BlockSpec((eb, d), lambda i, j: (j, 0)),
            pl.BlockSpec((d, eb), lambda i, j: (0, j)),
            part, full, part, full,
            pl.BlockSpec((PEER_HEADS, SUBLANES, tt), lambda i, j: (0, 0, i)),
        ],
        out_specs=pl.BlockSpec((d, tt), lambda i, j: (0, i)),
        out_shape=jax.ShapeDtypeStruct((d, n), F32),
        scratch_shapes=[pltpu.VMEM((eb, tt), BF16)],
        compiler_params=_cparams(("arbitrary", "arbitrary")),
        name="peer_dense",
    )(ht, u, vt, s1, s2, e1, e2, tau)


def _peer_res_kernel(x_ref, ot_ref, g_ref, o_ref):
    o_ref[...] = x_ref[...] + g_ref[...] * ot_ref[...].T


def _peer_res(x, out_t, mod, g_blk, tm):
    n, d = x.shape
    nseq, r, _ = mod.shape
    tiles_per_seq = (n // tm) // nseq
    return pl.pallas_call(
        _peer_res_kernel,
        grid=(n // tm,),
        in_specs=[pl.BlockSpec((tm, d), lambda i: (i, 0)),
                  pl.BlockSpec((d, tm), lambda i: (0, i)),
                  pl.BlockSpec((None, r, d), lambda i: (i // tiles_per_seq, 0, g_blk))],
        out_specs=pl.BlockSpec((tm, d), lambda i: (i, 0)),
        out_shape=jax.ShapeDtypeStruct((n, d), F32),
        compiler_params=_cparams(("arbitrary",)),
        name="peer_res",
    )(x, out_t, mod)


def _pad_cols(a, width):
    return jnp.pad(a, [(0, 0)] * (a.ndim - 1) + [(0, width - a.shape[-1])])


def _pad_rows(a, rows):
    return jnp.pad(a, [(0, rows - a.shape[0])] + [(0, 0)] * (a.ndim - 1))


def _layer_weights(l, W):
    bw = BRANCH_WIDTH
    w_in = W["w_in"][l]
    out = {
        "w_qkv": w_in[:, :RWKV_OFF].astype(BF16),
        "w_rw": _pad_cols(w_in[:, RWKV_OFF:SSM_OFF], RWKV_COLS_PAD).astype(BF16),
        "w_sm": _pad_cols(w_in[:, SSM_OFF:GATE_OFF], SSM_COLS_PAD).astype(BF16),
        "w_gate": w_in[:, GATE_OFF:].astype(BF16),
        "wb": W["w_branch"][l].astype(BF16),
        "wo": W["w_o"][l].astype(BF16),
        "wq": W["peer_wq"][l].astype(BF16),
        "u": W["peer_u"][l].astype(BF16),
        "vt": W["peer_v"][l].T.astype(BF16),
    }
    o1 = 3 * bw
    o2 = o1 + RWKV_DECAY_LORA
    o3 = o2 + RWKV_AAA_LORA
    rwp = {
        "mu": _pad_cols(W["rwkv_mu"][l][None, :], RWKV_COLS_PAD),
        "w0": W["rwkv_w0"][l][None, :],
        "w2": _pad_rows(W["rwkv_w2"][l], LANES).astype(BF16),
        "a0": W["rwkv_a0"][l][None, :],
        "a2": jnp.concatenate([jnp.zeros((RWKV_DECAY_LORA, bw), F32), W["rwkv_a2"][l]], axis=0).astype(BF16),
        "g2": _pad_rows(W["rwkv_g2"][l], 2 * LANES).astype(BF16),
        "kk": W["rwkv_kk"][l][None, :],
        "ka": W["rwkv_ka"][l][None, :],
        "rk": W["rwkv_rk"][l].reshape(1, bw),
        "lng": W["rwkv_lnx_g"][l][None, :],
        "lnb": W["rwkv_lnx_b"][l][None, :],
    }
    smp = {
        "cw": W["ssm_conv_w"][l],
        "cb": W["ssm_conv_b"][l][None, :],
        "dtb": _pad_cols(W["ssm_dt_bias"][l][None, :], LANES),
        "alog": _pad_cols(W["ssm_a_log"][l][None, :], LANES),
        "dvec": jnp.repeat(W["ssm_d"][l], SSM_HEAD_DIM)[None, :],
        "nrm": W["ssm_norm"][l][None, :],
    }
    lv = W["att_lambda"][l].astype(F32)
    lam_init = 0.8 - 0.6 * math.exp(-0.3 * l)
    lam = jnp.exp(jnp.sum(lv[0] * lv[1])) - jnp.exp(jnp.sum(lv[2] * lv[3])) + lam_init
    out.update(rwp=rwp, smp=smp, lam=lam, lam_init=lam_init)
    return out


def _layer(l, x, mod, lw, W, bsz, t, state, cfg, paged):
    n = bsz * t
    tm, tn = cfg["tm"], cfg["tn"]
    shift0, wkv0, conv0, ssm0 = state
    mm = lambda w, tn_: _normmm(x, W["norm1"][l], mod, 1, 0, w, tm, tn_)
    qkv = mm(lw["w_qkv"], 512)
    rw = mm(lw["w_rw"], 384)
    sm = mm(lw["w_sm"], 640)
    gates = mm(lw["w_gate"], 512)
    if paged is None:
        o_att = _attn_prompt(qkv, W["rel_bias"], lw["lam"], W["att_subln"][l], lw["lam_init"], bsz, t, cfg["tq"])
    else:
        cache_k, cache_v, page_table = paged
        o_att = _attn_sample(qkv, cache_k, cache_v, page_table, l, W["rel_bias"], lw["lam"], W["att_subln"][l],
                             lw["lam_init"], bsz, t)
    prep = _rwkv_prep(rw, shift0, lw["rwp"], bsz, t, cfg["rw_tm"])
    o_rwkv, wkv1 = _rwkv_scan(prep, wkv0, lw["rwp"]["lng"], lw["rwp"]["lnb"], bsz, t, cfg["rw_bb"], cfg["rw_tc"],
                               cfg["rw_pg"])
    o_ssm, ssm1 = _mamba(sm, conv0, ssm0, lw["smp"], bsz, t)
    merged = _merge(o_att, o_rwkv, o_ssm, gates, lw["wb"], cfg["tm2"], tn)
    x = _proj_res(merged, lw["wo"], x, mod, 2, cfg["tm2"], tn)
    npad = -(-n // LANES) * LANES
    xq, modq, tmq = x, mod, tm
    if npad != n:
        xq = _pad_rows(x, npad)
        modq = jnp.pad(mod, ((0, 0), (0, npad - n), (0, 0)))
        tmq = npad
    q = _normmm(xq, W["norm2"][l], modq, 4, 3, lw["wq"], tmq, tn)
    ht = _normT(xq, W["norm2"][l], modq, 4, 3, cfg["tmT"])
    s1, s2, e1, e2, tau = _peer_route(q, W["peer_k1"][l], W["peer_k2"][l], cfg["tt_route"])
    out_t = _peer_dense(ht, lw["u"], lw["vt"], s1, s2, e1, e2, tau, cfg["tt_dense"], cfg["ib"])
    x = _peer_res(xq, out_t, modq, 5, cfg["tmT"])[:n]
    qkv3 = qkv.reshape(bsz, t, 3 * ATT_QK)
    k_new = qkv3[:, :, ATT_QK:2 * ATT_QK].reshape(bsz, t, ATT_HEADS, 2 * ATT_DH)
    v_new = qkv3[:, :, 2 * ATT_QK:].reshape(bsz, t, ATT_HEADS, ATT_VD)
    shift1 = rw.reshape(bsz, t, RWKV_COLS_PAD)[:, -1, :RWKV_COLS]
    assert t >= SSM_CONV - 1
    conv1 = sm.reshape(bsz, t, SSM_COLS_PAD)[:, -(SSM_CONV - 1):, BRANCH_WIDTH:BRANCH_WIDTH + SSM_CONV_DIM]
    return x, (k_new, v_new, shift1, wkv1, conv1, ssm1)


def _wkv_to_kernel(s):
    b = s.shape[0]
    s = s.reshape(b, RWKV_HEADS // 2, 2, RWKV_DH, RWKV_DH)
    return jnp.transpose(s, (0, 1, 3, 2, 4)).reshape(b, RWKV_HEADS // 2, RWKV_DH, 2 * RWKV_DH)


def _wkv_from_kernel(s):
    b = s.shape[0]
    s = s.reshape(b, RWKV_HEADS // 2, RWKV_DH, 2, RWKV_DH)
    return jnp.transpose(s, (0, 1, 3, 2, 4)).reshape(b, RWKV_HEADS, RWKV_DH, RWKV_DH)


def _run_group(x, mod_of_layer, W, lws, bsz, t, states, cfg, paged_of_layer):
    xs = x.reshape(bsz * t, D_MODEL)
    outs = []
    for l in range(DEPTH):
        xs, new = _layer(l, xs, mod_of_layer(l), lws[l], W, bsz, t, states[l], cfg, paged_of_layer(l))
        outs.append(new)
    y = _final_norm(xs, W["norm_f"], cfg["tm_f"]).reshape(bsz, t, D_MODEL)
    k = jnp.stack([o[0] for o in outs], axis=1)
    v = jnp.stack([o[1] for o in outs], axis=1)
    shift = jnp.stack([o[2] for o in outs], axis=0)
    wkv = jnp.stack([_wkv_from_kernel(o[3]) for o in outs], axis=0)
    conv = jnp.stack([o[4] for o in outs], axis=0)
    ssm = jnp.stack([o[5].reshape(bsz, SSM_HEADS, SSM_HEAD_DIM, SSM_STATE) for o in outs], axis=0)
    return y, k, v, shift, wkv, conv, ssm


PROMPT_CFG = dict(tm=1024, tn=512, tm2=512, tq=256, rw_tm=256, rw_bb=2, rw_tc=128, rw_pg=4, tmT=256,
                  tt_route=256, tt_dense=512, ib=8, tm_f=512)


def _sample_cfg(n, t):
    npad = -(-n // LANES) * LANES
    return dict(tm=n, tn=512, tm2=n, tq=None, rw_tm=t, rw_bb=2, rw_tc=t, rw_pg=4, tmT=npad,
                tt_route=npad, tt_dense=npad, ib=8, tm_f=n)


def kernel(x_prompt, x_sample, cache_k, cache_v, state_rwkv_shift, state_rwkv, state_conv, state_ssm, page_table, c_prompt, c_sample, w_ada, b_ada, norm1, norm2, norm_f, w_in, rel_bias, att_lambda, att_subln, rwkv_mu, rwkv_w0, rwkv_w2, rwkv_a0, rwkv_a2, rwkv_g2, rwkv_kk, rwkv_ka, rwkv_rk, rwkv_lnx_g, rwkv_lnx_b, ssm_conv_w, ssm_conv_b, ssm_dt_bias, ssm_a_log, ssm_d, ssm_norm, w_branch, w_o, peer_wq, peer_k1, peer_k2, peer_u, peer_v):
    W = dict(norm1=norm1, norm2=norm2, norm_f=norm_f, w_in=w_in, rel_bias=rel_bias, att_lambda=att_lambda,
             att_subln=att_subln, rwkv_mu=rwkv_mu, rwkv_w0=rwkv_w0, rwkv_w2=rwkv_w2, rwkv_a0=rwkv_a0,
             rwkv_a2=rwkv_a2, rwkv_g2=rwkv_g2, rwkv_kk=rwkv_kk, rwkv_ka=rwkv_ka, rwkv_rk=rwkv_rk,
             rwkv_lnx_g=rwkv_lnx_g, rwkv_lnx_b=rwkv_lnx_b, ssm_conv_w=ssm_conv_w, ssm_conv_b=ssm_conv_b,
             ssm_dt_bias=ssm_dt_bias, ssm_a_log=ssm_a_log, ssm_d=ssm_d, ssm_norm=ssm_norm, w_branch=w_branch,
             w_o=w_o, peer_wq=peer_wq, peer_k1=peer_k1, peer_k2=peer_k2, peer_u=peer_u, peer_v=peer_v)
    bp, tp, _ = x_prompt.shape
    bs, ts, _ = x_sample.shape
    nseq = bp + bs
    rows = -(-nseq // SUBLANES) * SUBLANES
    c_all = _pad_rows(jnp.concatenate([c_prompt, c_sample], axis=0), rows)
    mod = _ada(c_all, w_ada, b_ada)
    lws = [_layer_weights(l, W) for l in range(DEPTH)]
    npair = RWKV_HEADS // 2
    zero_state = (jnp.zeros((bp, RWKV_COLS_PAD), F32),
                  jnp.zeros((bp, npair, RWKV_DH, LANES), F32),
                  jnp.zeros((bp, SUBLANES, SSM_CONV_DIM), F32),
                  jnp.zeros((bp, SSM_HEADS // 2, LANES, SSM_STATE), F32))
    outs_p = _run_group(x_prompt, lambda l: mod[l, :bp].reshape(bp, 1, 6 * D_MODEL), W, lws, bp, tp,
                        [zero_state] * DEPTH, PROMPT_CFG, lambda l: None)
    states_s = []
    for l in range(DEPTH):
        states_s.append((
            _pad_cols(state_rwkv_shift[l], RWKV_COLS_PAD),
            _wkv_to_kernel(state_rwkv[l]),
            jnp.pad(state_conv[l], ((0, 0), (SUBLANES - (SSM_CONV - 1), 0), (0, 0))),
            state_ssm[l].reshape(bs, SSM_HEADS // 2, LANES, SSM_STATE),
        ))
    ns = bs * ts
    mod_s = lambda l: jnp.repeat(mod[l, bp:bp + bs], ts, axis=0).reshape(1, ns, 6 * D_MODEL)
    outs_s = _run_group(x_sample, mod_s, W, lws, bs, ts, states_s, _sample_cfg(ns, ts),
                        lambda l: (cache_k, cache_v, page_table))
    return (outs_p[0], outs_s[0], *outs_p[1:], *outs_s[1:])
```

```python
import functools
import math

import jax
import jax.numpy as jnp
import numpy as np
from jax import lax
from jax.experimental import pallas as pl
from jax.experimental.pallas import tpu as pltpu

F32 = jnp.float32
BF16 = jnp.bfloat16

LANES = 128
SUBLANES = 8
VMEM_LIMIT_BYTES = 56 * 1024 * 1024

D_MODEL = 2048
DEPTH = 4
PAGE_SIZE = 128
BRANCH_WIDTH = D_MODEL // 2
N_BRANCHES = 3
ATT_DH = 64
ATT_VD = 2 * ATT_DH
ATT_HEADS = BRANCH_WIDTH // ATT_VD
ATT_QK = ATT_HEADS * 2 * ATT_DH
REL_BUCKETS = 32
REL_MAX_DIST = 128
RWKV_DH = 64
RWKV_HEADS = BRANCH_WIDTH // RWKV_DH
RWKV_DECAY_LORA = 64
RWKV_AAA_LORA = 64
RWKV_GATE_LORA = 160
RWKV_LN_EPS = 64e-5
RWKV_COLS = 3 * BRANCH_WIDTH + RWKV_DECAY_LORA + RWKV_AAA_LORA + RWKV_GATE_LORA
RWKV_COLS_PAD = 3456
SSM_HEAD_DIM = 64
SSM_HEADS = BRANCH_WIDTH // SSM_HEAD_DIM
SSM_GROUPS = 4
SSM_STATE = 128
SSM_CONV = 4
SSM_CONV_DIM = BRANCH_WIDTH + 2 * SSM_GROUPS * SSM_STATE
SSM_CHUNK = 128
SSM_COLS = BRANCH_WIDTH + SSM_CONV_DIM + SSM_HEADS
SSM_COLS_PAD = 3200
PEER_HEADS = 8
PEER_NKEYS = 128
PEER_EXPERTS = PEER_NKEYS * PEER_NKEYS
PEER_KEY_DIM = 256
PEER_HALF = PEER_KEY_DIM // 2
PEER_TOPK = 16
ATT_Q_OFF = 0
ATT_K_OFF = ATT_QK
ATT_V_OFF = 2 * ATT_QK
RWKV_OFF = ATT_V_OFF + ATT_HEADS * ATT_VD
SSM_OFF = RWKV_OFF + RWKV_COLS
GATE_OFF = SSM_OFF + SSM_COLS

NEG_BIG = -1e30


def _cparams(semantics):
    return pltpu.CompilerParams(dimension_semantics=semantics, vmem_limit_bytes=VMEM_LIMIT_BYTES)


def _dot(a, b):
    return jnp.dot(a, b, preferred_element_type=F32)


def _dot_nt(a, b):
    return lax.dot_general(a, b, (((1,), (1,)), ((), ())), preferred_element_type=F32)


def _dot_tn(a, b):
    return lax.dot_general(a, b, (((0,), (0,)), ((), ())), preferred_element_type=F32)


def _split2(x):
    hi = x.astype(BF16)
    lo = (x - hi.astype(F32)).astype(BF16)
    return hi, lo


def _split3(x):
    hi = x.astype(BF16)
    r = x - hi.astype(F32)
    mid = r.astype(BF16)
    lo = (r - mid.astype(F32)).astype(BF16)
    return hi, mid, lo


def _dot_x3(x, w_bf16):
    hi, mid, lo = _split3(x)
    return _dot(hi, w_bf16) + _dot(mid, w_bf16) + _dot(lo, w_bf16)


def _dot_x2(x, w_bf16):
    hi, lo = _split2(x)
    return _dot(hi, w_bf16) + _dot(lo, w_bf16)


def _block_ones(n, width):
    r = np.arange(n) // width
    return jnp.asarray((r[:, None] == r[None, :]).astype(np.float32), dtype=BF16)


def _segsum_bcast(x, ones_ref, width_block):
    cols = x.shape[1]
    outs = []
    e = ones_ref[...]
    for c in range(cols // width_block):
        outs.append(_dot_x2(x[:, c * width_block:(c + 1) * width_block], e))
    return outs[0] if len(outs) == 1 else jnp.concatenate(outs, axis=1)


def _silu(x):
    return x * jax.nn.sigmoid(x)


def _softplus(x):
    return jnp.maximum(x, 0.0) + jnp.log1p(jnp.exp(-jnp.abs(x)))


def _ada_kernel(c_ref, w_ref, b_ref, o_ref):
    c = c_ref[...]
    a = _silu(c).astype(BF16)
    o_ref[...] = _dot(a, w_ref[...].astype(BF16)) + b_ref[...]


def _ada(c_all, w_ada, b_ada):
    rows = c_all.shape[0]
    tn = 1024
    ncol = w_ada.shape[2]
    return pl.pallas_call(
        _ada_kernel,
        grid=(DEPTH, ncol // tn),
        in_specs=[
            pl.BlockSpec((rows, D_MODEL), lambda l, j: (0, 0)),
            pl.BlockSpec((None, D_MODEL, tn), lambda l, j: (l, 0, j)),
            pl.BlockSpec((None, 1, tn), lambda l, j: (l, 0, j)),
        ],
        out_specs=pl.BlockSpec((None, rows, tn), lambda l, j: (l, 0, j)),
        out_shape=jax.ShapeDtypeStruct((DEPTH, rows, ncol), F32),
        compiler_params=_cparams(("arbitrary", "arbitrary")),
        name="ada",
    )(c_all, w_ada, b_ada.reshape(DEPTH, 1, ncol))


def _norm_rows(x, g, eps):
    return x * lax.rsqrt(jnp.mean(x * x, axis=-1, keepdims=True) + eps) * g


def _normmm_kernel(x_ref, g_ref, sc_ref, sh_ref, w_ref, o_ref, h_scr):
    @pl.when(pl.program_id(1) == 0)
    def _():
        h = _norm_rows(x_ref[...], g_ref[...], 1e-6) * (1.0 + sc_ref[...]) + sh_ref[...]
        h_scr[...] = h.astype(BF16)

    o_ref[...] = _dot(h_scr[...], w_ref[...]).astype(o_ref.dtype)


def _normmm(x, g, mod, sc_blk, sh_blk, w, tm, tn, out_dtype=F32):
    n, d = x.shape
    ncol = w.shape[1]
    nseq, r, _ = mod.shape
    tiles_per_seq = (n // tm) // nseq
    return pl.pallas_call(
        _normmm_kernel,
        grid=(n // tm, ncol // tn),
        in_specs=[
            pl.BlockSpec((tm, d), lambda i, j: (i, 0)),
            pl.BlockSpec((1, d), lambda i, j: (0, 0)),
            pl.BlockSpec((None, r, d), lambda i, j: (i // tiles_per_seq, 0, sc_blk)),
            pl.BlockSpec((None, r, d), lambda i, j: (i // tiles_per_seq, 0, sh_blk)),
            pl.BlockSpec((d, tn), lambda i, j: (0, j)),
        ],
        out_specs=pl.BlockSpec((tm, tn), lambda i, j: (i, j)),
        out_shape=jax.ShapeDtypeStruct((n, ncol), out_dtype),
        scratch_shapes=[pltpu.VMEM((tm, d), BF16)],
        compiler_params=_cparams(("arbitrary", "arbitrary")),
        name="normmm",
    )(x, g.reshape(1, d), mod, mod, w)


def _normT_kernel(x_ref, g_ref, sc_ref, sh_ref, o_ref):
    h = _norm_rows(x_ref[...], g_ref[...], 1e-6) * (1.0 + sc_ref[...]) + sh_ref[...]
    o_ref[...] = h.T.astype(BF16)


def _normT(x, g, mod, sc_blk, sh_blk, tm):
    n, d = x.shape
    nseq, r, _ = mod.shape
    tiles_per_seq = (n // tm) // nseq
    return pl.pallas_call(
        _normT_kernel,
        grid=(n // tm,),
        in_specs=[
            pl.BlockSpec((tm, d), lambda i: (i, 0)),
            pl.BlockSpec((1, d), lambda i: (0, 0)),
            pl.BlockSpec((None, r, d), lambda i: (i // tiles_per_seq, 0, sc_blk)),
            pl.BlockSpec((None, r, d), lambda i: (i // tiles_per_seq, 0, sh_blk)),
        ],
        out_specs=pl.BlockSpec((d, tm), lambda i: (0, i)),
        out_shape=jax.ShapeDtypeStruct((d, n), BF16),
        compiler_params=_cparams(("arbitrary",)),
        name="normT",
    )(x, g.reshape(1, d), mod, mod)


def _final_norm_kernel(x_ref, g_ref, o_ref):
    o_ref[...] = _norm_rows(x_ref[...], g_ref[...], 1e-6)


def _final_norm(x, g, tm):
    n, d = x.shape
    return pl.pallas_call(
        _final_norm_kernel,
        grid=(n // tm,),
        in_specs=[pl.BlockSpec((tm, d), lambda i: (i, 0)), pl.BlockSpec((1, d), lambda i: (0, 0))],
        out_specs=pl.BlockSpec((tm, d), lambda i: (i, 0)),
        out_shape=jax.ShapeDtypeStruct((n, d), F32),
        compiler_params=_cparams(("arbitrary",)),
        name="final_norm",
    )(x, g.reshape(1, d))


def _merge_kernel(o0_ref, o1_ref, o2_ref, g0_ref, g1_ref, g2_ref, w_ref, out_ref):
    acc = jax.nn.sigmoid(g0_ref[...].astype(F32)) * _dot(o0_ref[...], w_ref[0])
    acc += jax.nn.sigmoid(g1_ref[...].astype(F32)) * _dot(o1_ref[...], w_ref[1])
    acc += jax.nn.sigmoid(g2_ref[...].astype(F32)) * _dot(o2_ref[...], w_ref[2])
    out_ref[...] = acc.astype(out_ref.dtype)


def _merge(o_att, o_rwkv, o_ssm, gates, wb, tm, tn):
    n = o_att.shape[0]
    gblk = D_MODEL // tn
    o_spec = pl.BlockSpec((tm, BRANCH_WIDTH), lambda i, j: (i, 0))
    g_specs = [pl.BlockSpec((tm, tn), functools.partial(lambda i, j, b: (i, b * gblk + j), b=b)) for b in range(3)]
    return pl.pallas_call(
        _merge_kernel,
        grid=(n // tm, D_MODEL // tn),
        in_specs=[o_spec, o_spec, o_spec, *g_specs,
                  pl.BlockSpec((N_BRANCHES, BRANCH_WIDTH, tn), lambda i, j: (0, 0, j))],
        out_specs=pl.BlockSpec((tm, tn), lambda i, j: (i, j)),
        out_shape=jax.ShapeDtypeStruct((n, D_MODEL), BF16),
        compiler_params=_cparams(("arbitrary", "arbitrary")),
        name="merge",
    )(o_att, o_rwkv, o_ssm, gates, gates, gates, wb)


def _proj_res_kernel(a_ref, w_ref, x_ref, g_ref, o_ref):
    o_ref[...] = x_ref[...] + g_ref[...] * _dot(a_ref[...], w_ref[...])


def _proj_res(a, w, x, mod, g_blk, tm, tn):
    n, k = a.shape
    nseq, r, _ = mod.shape
    tiles_per_seq = (n // tm) // nseq
    nblk = D_MODEL // tn
    return pl.pallas_call(
        _proj_res_kernel,
        grid=(n // tm, D_MODEL // tn),
        in_specs=[
            pl.BlockSpec((tm, k), lambda i, j: (i, 0)),
            pl.BlockSpec((k, tn), lambda i, j: (0, j)),
            pl.BlockSpec((tm, tn), lambda i, j: (i, j)),
            pl.BlockSpec((None, r, tn), lambda i, j: (i // tiles_per_seq, 0, g_blk * nblk + j)),
        ],
        out_specs=pl.BlockSpec((tm, tn), lambda i, j: (i, j)),
        out_shape=jax.ShapeDtypeStruct((n, D_MODEL), F32),
        compiler_params=_cparams(("arbitrary", "arbitrary")),
        name="proj_res",
    )(a, w, x, mod)


def _rel_bucket(dist):
    n = jnp.maximum(dist, 0)
    max_exact = REL_BUCKETS // 2
    nf = jnp.maximum(n, 1).astype(F32)
    large = max_exact + (jnp.log(nf / max_exact) / math.log(REL_MAX_DIST / max_exact)
                         * (REL_BUCKETS - max_exact)).astype(jnp.int32)
    large = jnp.minimum(large, REL_BUCKETS - 1)
    return jnp.where(n < max_exact, n, large)


def _bias_of_dist(rel_bias, dist):
    b = jnp.moveaxis(rel_bias[_rel_bucket(dist)].astype(F32), -1, 0)
    return jnp.where(dist[None] >= 0, b, NEG_BIG)


def _softmax_step(s, v_bf16, m_scr, l_scr, acc_scr):
    width = min(LANES, s.shape[1])
    cols = [s[:, c:c + width] for c in range(0, s.shape[1], width)]
    m_prev = m_scr[...]
    m_new = jnp.maximum(m_prev, jnp.max(functools.reduce(jnp.maximum, cols), axis=-1, keepdims=True))
    alpha = jnp.exp(m_prev - m_new)
    ps = [jnp.exp(c - m_new[:, :width]) for c in cols]
    l_scr[...] = alpha * l_scr[...] + jnp.sum(functools.reduce(jnp.add, ps), axis=-1, keepdims=True)
    p = ps[0] if len(ps) == 1 else jnp.concatenate(ps, axis=1)
    acc_scr[...] = alpha * acc_scr[...] + _dot(p.astype(BF16), v_bf16)
    m_scr[...] = m_new


def _two_map_queries(q, scale):
    lane = lax.broadcasted_iota(jnp.int32, q.shape, 1)
    qs = q * scale
    q0 = jnp.where(lane < ATT_DH, qs, 0.0)
    q1 = jnp.where(lane >= ATT_DH, qs, 0.0)
    return jnp.concatenate([q0, q1], axis=0).astype(BF16)


def _diff_finalize(acc, l, lam, g, rows, post_scale):
    o0 = acc[:rows] / l[:rows]
    o1 = acc[rows:] / l[rows:]
    o = o0 - lam * o1
    o = o * lax.rsqrt(jnp.mean(o * o, axis=-1, keepdims=True) + 1e-5) * g
    return o * post_scale


def _attn_prompt_kernel(qi_ref, ki_ref, lam_ref, q_ref, k_ref, v_ref, bias_ref, g_ref, o_ref,
                        m_scr, l_scr, acc_scr, *, tq, post_scale):
    p = pl.program_id(2)
    qi = qi_ref[p]
    ki = ki_ref[p]

    @pl.when(ki == 0)
    def _():
        m_scr[...] = jnp.full(m_scr.shape, NEG_BIG, F32)
        l_scr[...] = jnp.zeros(l_scr.shape, F32)
        acc_scr[...] = jnp.zeros(acc_scr.shape, F32)

    qq = _two_map_queries(q_ref[...], ATT_DH ** -0.5)
    s = _dot_nt(qq, k_ref[...].astype(BF16))
    b = bias_ref[...]
    s = s + jnp.concatenate([b, b], axis=0)
    _softmax_step(s, v_ref[...].astype(BF16), m_scr, l_scr, acc_scr)

    @pl.when(ki == qi)
    def _():
        o = _diff_finalize(acc_scr[...], l_scr[...], lam_ref[0, 0], g_ref[...], tq, post_scale)
        o_ref[...] = o.astype(o_ref.dtype)


def _attn_prompt(qkv, rel_bias, lam, subln, lam_init, bsz, t, tq):
    nq = t // tq
    pairs = [(a, b) for a in range(nq) for b in range(a + 1)]
    qi_tab = jnp.asarray([a for a, _ in pairs], jnp.int32)
    ki_tab = jnp.asarray([b for _, b in pairs], jnp.int32)
    assert tq >= REL_MAX_DIST
    ii = jnp.arange(tq, dtype=jnp.int32)
    d0 = ii[:, None] - ii[None, :]
    tiles = jnp.stack([_bias_of_dist(rel_bias, d0), _bias_of_dist(rel_bias, d0 + tq),
                       _bias_of_dist(rel_bias, d0 + 2 * tq)], axis=1)
    qkv3 = qkv.reshape(bsz, t, 3 * ATT_QK)
    kern = functools.partial(_attn_prompt_kernel, tq=tq, post_scale=1.0 - lam_init)
    grid_spec = pltpu.PrefetchScalarGridSpec(
        num_scalar_prefetch=2,
        grid=(bsz, ATT_HEADS, len(pairs)),
        in_specs=[
            pl.BlockSpec(memory_space=pltpu.SMEM),
            pl.BlockSpec((None, tq, LANES), lambda b, h, p, qt, kt: (b, qt[p], h)),
            pl.BlockSpec((None, tq, LANES), lambda b, h, p, qt, kt: (b, kt[p], ATT_HEADS + h)),
            pl.BlockSpec((None, tq, LANES), lambda b, h, p, qt, kt: (b, kt[p], 2 * ATT_HEADS + h)),
            pl.BlockSpec((None, None, tq, tq), lambda b, h, p, qt, kt: (h, jnp.minimum(qt[p] - kt[p], 2), 0, 0)),
            pl.BlockSpec((1, LANES), lambda b, h, p, qt, kt: (0, 0)),
        ],
        out_specs=pl.BlockSpec((None, tq, LANES), lambda b, h, p, qt, kt: (b, qt[p], h)),
        scratch_shapes=[pltpu.VMEM((2 * tq, LANES), F32), pltpu.VMEM((2 * tq, LANES), F32),
                        pltpu.VMEM((2 * tq, LANES), F32)],
    )
    out = pl.pallas_call(
        kern,
        grid_spec=grid_spec,
        out_shape=jax.ShapeDtypeStruct((bsz, t, BRANCH_WIDTH), BF16),
        compiler_params=_cparams(("arbitrary", "arbitrary", "arbitrary")),
        name="attn_prompt",
    )(qi_tab, ki_tab, lam.reshape(1, 1), qkv3, qkv3, qkv3, tiles, subln.reshape(1, ATT_VD))
    return out.reshape(bsz * t, BRANCH_WIDTH)


def _attn_sample_kernel(pt_ref, lam_ref, q_ref, kn_ref, vn_ref, bias_ref, biasn_ref, g_ref, *rest,
                        pps, rows, post_scale):
    kp_refs = rest[:pps]
    vp_refs = rest[pps:2 * pps]
    o_ref, m_scr, l_scr, acc_scr = rest[2 * pps:]
    s_idx = pl.program_id(1)
    last = s_idx == pl.num_programs(1) - 1

    @pl.when(s_idx == 0)
    def _():
        m_scr[...] = jnp.full(m_scr.shape, NEG_BIG, F32)
        l_scr[...] = jnp.zeros(l_scr.shape, F32)
        acc_scr[...] = jnp.zeros(acc_scr.shape, F32)

    qq = _two_map_queries(q_ref[...], ATT_DH ** -0.5)
    for r in range(pps):
        kb = kp_refs[r][...].reshape(PAGE_SIZE * ATT_HEADS, LANES).astype(BF16)
        vb = vp_refs[r][...].reshape(PAGE_SIZE * ATT_HEADS, LANES).astype(BF16)
        s = _dot_nt(qq, kb)
        if r == pps - 1:
            b = jnp.where(last, bias_ref[1], bias_ref[0])
        else:
            b = bias_ref[0]
        s = s + jnp.concatenate([b, b], axis=0)
        _softmax_step(s, vb, m_scr, l_scr, acc_scr)

    @pl.when(last)
    def _():
        s = _dot_nt(qq, kn_ref[...].astype(BF16))
        b = biasn_ref[...]
        s = s + jnp.concatenate([b, b], axis=0)
        _softmax_step(s, vn_ref[...].astype(BF16), m_scr, l_scr, acc_scr)
        o = _diff_finalize(acc_scr[...], l_scr[...], lam_ref[0, 0], g_ref[...], rows, post_scale)
        o_ref[...] = o.astype(o_ref.dtype)


def _attn_sample(qkv, cache_k, cache_v, page_table, layer, rel_bias, lam, subln, lam_init, bsz, t):
    n_pages = page_table.shape[1]
    past = n_pages * PAGE_SIZE
    pps = 4
    assert n_pages % pps == 0
    rows = t * ATT_HEADS
    q = qkv[:, :ATT_QK].reshape(bsz, rows, LANES)
    kn = qkv[:, ATT_QK:2 * ATT_QK].reshape(bsz, rows, LANES)
    vn = qkv[:, 2 * ATT_QK:].reshape(bsz, rows, LANES)
    tok = jnp.repeat(jnp.arange(t, dtype=jnp.int32), ATT_HEADS)
    hq = jnp.tile(jnp.arange(ATT_HEADS, dtype=jnp.int32), t)
    kk = jnp.repeat(jnp.arange(PAGE_SIZE, dtype=jnp.int32), ATT_HEADS)
    hk = jnp.tile(jnp.arange(ATT_HEADS, dtype=jnp.int32), PAGE_SIZE)

    def tile(dist, hk_):
        b = rel_bias[_rel_bucket(dist), hq[:, None]].astype(F32)
        return jnp.where((dist >= 0) & (hq[:, None] == hk_[None, :]), b, NEG_BIG)

    assert PAGE_SIZE >= REL_MAX_DIST
    far = tile(jnp.broadcast_to(jnp.int32(2 * PAGE_SIZE), (rows, PAGE_SIZE * ATT_HEADS)) + tok[:, None] - kk[None, :], hk)
    near = tile(PAGE_SIZE + tok[:, None] - kk[None, :], hk)
    bias_pages = jnp.stack([far, near])
    bias_new = tile(tok[:, None] - tok[None, :], hq)
    kern = functools.partial(_attn_sample_kernel, pps=pps, rows=rows, post_scale=1.0 - lam_init)
    page_block = (None, None, PAGE_SIZE, ATT_HEADS, LANES)
    page_specs = [pl.BlockSpec(page_block, functools.partial(lambda b, s, pt, r: (pt[b, s * pps + r], layer, 0, 0, 0), r=r))
                  for r in range(pps)]
    grid_spec = pltpu.PrefetchScalarGridSpec(
        num_scalar_prefetch=1,
        grid=(bsz, n_pages // pps),
        in_specs=[
            pl.BlockSpec(memory_space=pltpu.SMEM),
            pl.BlockSpec((None, rows, LANES), lambda b, s, pt: (b, 0, 0)),
            pl.BlockSpec((None, rows, LANES), lambda b, s, pt: (b, 0, 0)),
            pl.BlockSpec((None, rows, LANES), lambda b, s, pt: (b, 0, 0)),
            pl.BlockSpec((2, rows, PAGE_SIZE * ATT_HEADS), lambda b, s, pt: (0, 0, 0)),
            pl.BlockSpec((rows, rows), lambda b, s, pt: (0, 0)),
            pl.BlockSpec((1, LANES), lambda b, s, pt: (0, 0)),
            *page_specs, *page_specs,
        ],
        out_specs=pl.BlockSpec((None, rows, LANES), lambda b, s, pt: (b, 0, 0)),
        scratch_shapes=[pltpu.VMEM((2 * rows, LANES), F32), pltpu.VMEM((2 * rows, LANES), F32),
                        pltpu.VMEM((2 * rows, LANES), F32)],
    )
    out = pl.pallas_call(
        kern,
        grid_spec=grid_spec,
        out_shape=jax.ShapeDtypeStruct((bsz, rows, LANES), BF16),
        compiler_params=_cparams(("arbitrary", "arbitrary")),
        name="attn_sample",
    )(page_table, lam.reshape(1, 1), q, kn, vn, bias_pages, bias_new, subln.reshape(1, ATT_VD),
      *([cache_k] * pps), *([cache_v] * pps))
    return out.reshape(bsz * t, BRANCH_WIDTH)


RW_LORA_OFF = 3 * BRANCH_WIDTH
RW_G_OFF = RW_LORA_OFF + LANES


def _rwkv_prep_kernel(x_ref, prev_ref, shift_ref, mu_ref, w0_ref, w2_ref, a0_ref, a2_ref, g2_ref,
                      kkw_ref, ka_ref, rk_ref, e_ref,
                      a_out, wr_out, w_out, b_out, k_out, br_out, kr_out, v_out, bon_out, g_out):
    bw = BRANCH_WIDTH
    x = x_ref[...]
    tm = x.shape[0]
    first = pl.program_id(1) == 0
    prev_row = jnp.where(first, shift_ref[...], prev_ref[SUBLANES - 1:SUBLANES, :])
    row = lax.broadcasted_iota(jnp.int32, x.shape, 0)
    prev = jnp.where(row == 0, prev_row, pltpu.roll(x, 1, axis=0))
    pm = x + (prev - x) * mu_ref[...]
    r = pm[:, :bw]
    k = pm[:, bw:2 * bw]
    v = pm[:, 2 * bw:3 * bw]
    lora = pm[:, RW_LORA_OFF:RW_LORA_OFF + LANES]
    glo = pm[:, RW_G_OFF:RW_G_OFF + 2 * LANES]
    w = -_softplus(-(w0_ref[...] + _dot(jnp.tanh(lora).astype(BF16), w2_ref[...]))) - 0.5
    decay = jnp.exp(-jnp.exp(w))
    a = jax.nn.sigmoid(a0_ref[...] + _dot(lora.astype(BF16), a2_ref[...]))
    g = _dot(jax.nn.sigmoid(glo).astype(BF16), g2_ref[...])
    kk = k * kkw_ref[...]
    n2 = _segsum_bcast(kk * kk, e_ref, LANES)
    kk = kk / jnp.maximum(jnp.sqrt(n2), 1e-12)
    k2 = k * (1.0 + (a - 1.0) * ka_ref[...])
    bm = kk * a
    a_out[...] = -kk
    wr_out[...] = decay * r
    w_out[...] = decay
    b_out[...] = bm
    k_out[...] = k2
    br_out[...] = _segsum_bcast(bm * r, e_ref, LANES)
    kr_out[...] = _segsum_bcast(k2 * r, e_ref, LANES)
    v_out[...] = v
    bon_out[...] = _segsum_bcast(r * k2 * rk_ref[...], e_ref, LANES) * v
    g_out[...] = g


def _rwkv_prep(rw, shift0, P, bsz, t, tm):
    n = rw.shape[0]
    bw = BRANCH_WIDTH
    cp = RWKV_COLS_PAD
    tiles = t // tm
    vec = lambda nm: pl.BlockSpec((1, bw), lambda b, i: (0, 0))
    out_spec = pl.BlockSpec((tm, bw), lambda b, i: (b * tiles + i, 0))
    rows8 = tm // SUBLANES
    outs = pl.pallas_call(
        _rwkv_prep_kernel,
        grid=(bsz, tiles),
        in_specs=[
            pl.BlockSpec((tm, cp), lambda b, i: (b * tiles + i, 0)),
            pl.BlockSpec((SUBLANES, cp), lambda b, i: (jnp.maximum((b * tiles + i) * rows8 - 1, 0), 0)),
            pl.BlockSpec((None, 1, cp), lambda b, i: (b, 0, 0)),
            pl.BlockSpec((1, cp), lambda b, i: (0, 0)),
            vec("w0"),
            pl.BlockSpec((LANES, bw), lambda b, i: (0, 0)),
            vec("a0"),
            pl.BlockSpec((LANES, bw), lambda b, i: (0, 0)),
            pl.BlockSpec((2 * LANES, bw), lambda b, i: (0, 0)),
            vec("kk"), vec("ka"), vec("rk"),
            pl.BlockSpec((LANES, LANES), lambda b, i: (0, 0)),
        ],
        out_specs=[out_spec] * 10,
        out_shape=[jax.ShapeDtypeStruct((n, bw), F32)] * 10,
        compiler_params=_cparams(("arbitrary", "arbitrary")),
        name="rwkv_prep",
    )(rw, rw, shift0.reshape(bsz, 1, cp), P["mu"], P["w0"], P["w2"], P["a0"], P["a2"], P["g2"],
      P["kk"], P["ka"], P["rk"], _block_ones(LANES, RWKV_DH))
    return outs


def _rwkv_scan_kernel(a_ref, wr_ref, w_ref, b_ref, k_ref, br_ref, kr_ref, v_ref, bon_ref, g_ref,
                      s0_ref, lng_ref, lnb_ref, e_ref, o_ref, sT_ref, s_scr, y_scr, *, bb, tc, pg):
    c = pl.program_id(2)

    @pl.when(c == 0)
    def _():
        s_scr[...] = s0_ref[...]

    e = e_ref[...]
    sub = lax.broadcasted_iota(jnp.int32, (RWKV_DH, LANES), 0)
    lane = lax.broadcasted_iota(jnp.int32, (RWKV_DH, LANES), 1)
    diag = (lane % RWKV_DH) == sub

    chains = [(b, p) for b in range(bb) for p in range(pg)]

    def group(tg, carry):
        t0 = pl.multiple_of(tg * SUBLANES, SUBLANES)
        states = [s_scr[b, p] for b, p in chains]
        ys = [[] for _ in chains]
        for r in range(SUBLANES):
            for ci, (b, p) in enumerate(chains):
                cs = pl.ds(p * LANES, LANES)
                row = lambda ref: ref[b, pl.ds(t0, SUBLANES), cs][r:r + 1, :]
                s = states[ci]
                p1 = s * row(a_ref)
                p2 = s * row(wr_ref)
                z = jnp.where(diag, row(v_ref), 0.0)
                p1h, p1l = _split2(p1)
                lhs = jnp.concatenate([p1h, p1l, p2.astype(BF16), z.astype(BF16)], axis=0)
                res = _dot(lhs, e)
                sa = res[0:RWKV_DH] + res[RWKV_DH:2 * RWKV_DH]
                m2 = res[2 * RWKV_DH:3 * RWKV_DH]
                vb = res[3 * RWKV_DH:4 * RWKV_DH]
                states[ci] = s * row(w_ref) + sa * row(b_ref) + vb * row(k_ref)
                yb = m2 + sa * row(br_ref) + vb * row(kr_ref)
                ys[ci].append(jnp.sum(jnp.where(diag, yb, 0.0), axis=0, keepdims=True))
        for ci, (b, p) in enumerate(chains):
            s_scr[b, p] = states[ci]
            y_scr[b, pl.ds(t0, SUBLANES), pl.ds(p * LANES, LANES)] = jnp.concatenate(ys[ci], axis=0)
        return carry

    lax.fori_loop(0, tc // SUBLANES, group, 0)

    for b in range(bb):
        y = y_scr[b]
        mu = _segsum_bcast(y, e_ref, LANES) * (1.0 / RWKV_DH)
        d = y - mu
        var = _segsum_bcast(d * d, e_ref, LANES) * (1.0 / RWKV_DH)
        yn = d * lax.rsqrt(var + RWKV_LN_EPS) * lng_ref[...] + lnb_ref[...]
        o_ref[b] = ((yn + bon_ref[b]) * g_ref[b]).astype(o_ref.dtype)

    @pl.when(c == pl.num_programs(2) - 1)
    def _():
        sT_ref[...] = s_scr[...]


def _rwkv_scan(prep, s0, lng, lnb, bsz, t, bb, tc, pg):
    bw = BRANCH_WIDTH
    npair = RWKV_HEADS // 2
    assert tc % SUBLANES == 0 and npair % pg == 0
    wcols = pg * LANES
    seq = [z.reshape(bsz, t, bw) for z in prep]
    blk = pl.BlockSpec((bb, tc, wcols), lambda b, q, c: (b, c, q))
    st = pl.BlockSpec((bb, pg, RWKV_DH, LANES), lambda b, q, c: (b, q, 0, 0))
    vec = pl.BlockSpec((1, wcols), lambda b, q, c: (0, q))
    kern = functools.partial(_rwkv_scan_kernel, bb=bb, tc=tc, pg=pg)
    o, s_t = pl.pallas_call(
        kern,
        grid=(bsz // bb, npair // pg, t // tc),
        in_specs=[blk] * 10 + [st, vec, vec, pl.BlockSpec((LANES, LANES), lambda b, q, c: (0, 0))],
        out_specs=[blk, st],
        out_shape=[jax.ShapeDtypeStruct((bsz, t, bw), BF16),
                   jax.ShapeDtypeStruct((bsz, npair, RWKV_DH, LANES), F32)],
        scratch_shapes=[pltpu.VMEM((bb, pg, RWKV_DH, LANES), F32), pltpu.VMEM((bb, tc, wcols), F32)],
        compiler_params=_cparams(("arbitrary", "arbitrary", "arbitrary")),
        name="rwkv_scan",
    )(*seq, s0, lng, lnb, _block_ones(LANES, RWKV_DH))
    return o.reshape(bsz * t, bw), s_t


def _mamba_kernel(x_ref, conv0_ref, h0_ref, cw_ref, cb_ref, dtb_ref, alog_ref, dvec_ref, nrm_ref,
                  tri_ref, eye_ref, xp_ref, xpt_ref, e256_ref, o_ref, hT_ref, tail_scr, h_scr, *, ln):
    bw = BRANCH_WIDTH
    gn = SSM_GROUPS * SSM_STATE
    npair = SSM_HEADS // 2
    c = pl.program_id(1)

    @pl.when(c == 0)
    def _():
        tail_scr[...] = conv0_ref[...]
        h_scr[...] = h0_ref[...]

    blk = x_ref[...]
    z = blk[:, :bw]
    xbc = blk[:, bw:bw + SSM_CONV_DIM]
    dtr = blk[:, bw + SSM_CONV_DIM:bw + SSM_CONV_DIM + LANES]
    xpad = jnp.concatenate([tail_scr[...], xbc], axis=0)
    conv = cb_ref[...]
    for i in range(SSM_CONV):
        off = SUBLANES - (SSM_CONV - 1) + i
        conv = conv + xpad[off:off + ln] * cw_ref[i:i + 1, :]
    tail_scr[...] = xpad[ln:ln + SUBLANES]
    conv = _silu(conv)
    xs = conv[:, :bw]
    lane = lax.broadcasted_iota(jnp.int32, (ln, LANES), 1)
    dt = jnp.where(lane < SSM_HEADS, _softplus(dtr + dtb_ref[...]), 0.0)
    a_neg = -jnp.exp(alog_ref[...])
    a = dt * a_neg
    ah, am_, al = _split3(a)
    tri = tri_ref[...]
    a_cs = _dot(tri, ah) + _dot(tri, am_) + _dot(tri, al)
    ch, cm, cl = _split3(a_cs)
    eye = eye_ref[...]
    a_cs_t = _dot_nt(eye, ch) + _dot_nt(eye, cm) + _dot_nt(eye, cl)
    a_tot = a_cs[ln - 1:ln, :]
    xp = xp_ref[...]
    dt_x = _dot_x3(dt, xp)
    acs_x = _dot_x3(a_cs, xp)
    atot_x = _dot_x3(jnp.broadcast_to(a_tot, (SUBLANES, LANES)), xp)[0:1]
    th, tm_, tl = _split3(jnp.broadcast_to(a_tot, (SUBLANES, LANES)))
    xpt = xpt_ref[...]
    atot_col = (_dot_nt(xpt, th) + _dot_nt(xpt, tm_) + _dot_nt(xpt, tl))[:, 0:1]
    xc = xs * dt_x
    xd = xc * jnp.exp(atot_x - acs_x)
    row_i = lax.broadcasted_iota(jnp.int32, (ln, ln), 0)
    col_i = lax.broadcasted_iota(jnp.int32, (ln, ln), 1)
    tril = row_i >= col_i
    lane_h = lax.broadcasted_iota(jnp.int32, (ln, LANES), 1) < SSM_HEAD_DIM
    y_parts = []
    for g in range(SSM_GROUPS):
        bg = conv[:, bw + g * SSM_STATE:bw + (g + 1) * SSM_STATE].astype(BF16)
        cg = conv[:, bw + gn + g * SSM_STATE:bw + gn + (g + 1) * SSM_STATE].astype(BF16)
        cbm = _dot_nt(cg, bg)
        for pr in range(2):
            pi = g * 2 + pr
            cs = slice(pi * LANES, (pi + 1) * LANES)
            xcp = xc[:, cs]
            y = None
            for e in range(2):
                hd = 2 * pi + e
                seg = a_cs[:, hd:hd + 1] - a_cs_t[hd:hd + 1, :]
                wm = jnp.where(tril, cbm * jnp.exp(jnp.where(tril, seg, 0.0)), 0.0).astype(BF16)
                xm = jnp.where(lane_h if e == 0 else jnp.logical_not(lane_h), xcp, 0.0).astype(BF16)
                part = _dot(wm, xm)
                y = part if y is None else y + part
            hp = h_scr[pi]
            y = y + _dot_nt(cg, hp.astype(BF16)) * jnp.exp(acs_x[:, cs])
            y_parts.append(y)
            h_new = jnp.exp(atot_col[pi * LANES:(pi + 1) * LANES, :]) * hp + _dot_tn(xd[:, cs].astype(BF16), bg)
            h_scr[pi] = h_new
    y = jnp.concatenate(y_parts, axis=1)
    y = (y + dvec_ref[...] * xs) * _silu(z)
    gw = bw // SSM_GROUPS
    ms = _segsum_bcast(y * y, e256_ref, gw) * (1.0 / gw)
    y = y * lax.rsqrt(ms + 1e-5) * nrm_ref[...]
    o_ref[...] = y.astype(o_ref.dtype)

    @pl.when(c == pl.num_programs(1) - 1)
    def _():
        hT_ref[...] = h_scr[...]


def _mamba(sm, conv0, h0, P, bsz, t):
    bw = BRANCH_WIDTH
    ln = min(SSM_CHUNK, t)
    assert t % ln == 0
    nc = t // ln
    npair = SSM_HEADS // 2
    tri = jnp.asarray(np.tril(np.ones((ln, ln), np.float32)), dtype=BF16)
    eye = jnp.asarray(np.eye(LANES, dtype=np.float32), dtype=BF16)
    xp_np = (np.arange(LANES)[:, None] == (np.arange(bw)[None, :] // SSM_HEAD_DIM)).astype(np.float32)
    xp = jnp.asarray(xp_np, dtype=BF16)
    xpt = jnp.asarray(xp_np.T.copy(), dtype=BF16)
    const = lambda shape: pl.BlockSpec(shape, lambda b, c: tuple(0 for _ in shape))
    kern = functools.partial(_mamba_kernel, ln=ln)
    o, h_t = pl.pallas_call(
        kern,
        grid=(bsz, nc),
        in_specs=[
            pl.BlockSpec((ln, SSM_COLS_PAD), lambda b, c: (b * nc + c, 0)),
            pl.BlockSpec((None, SUBLANES, SSM_CONV_DIM), lambda b, c: (b, 0, 0)),
            pl.BlockSpec((None, npair, LANES, SSM_STATE), lambda b, c: (b, 0, 0, 0)),
            const((SSM_CONV, SSM_CONV_DIM)), const((1, SSM_CONV_DIM)), const((1, LANES)), const((1, LANES)),
            const((1, bw)), const((1, bw)), const((ln, ln)), const((LANES, LANES)),
            const((LANES, bw)), const((bw, LANES)), const((bw // SSM_GROUPS, bw // SSM_GROUPS)),
        ],
        out_specs=[pl.BlockSpec((ln, bw), lambda b, c: (b * nc + c, 0)),
                   pl.BlockSpec((None, npair, LANES, SSM_STATE), lambda b, c: (b, 0, 0, 0))],
        out_shape=[jax.ShapeDtypeStruct((bsz * t, bw), BF16),
                   jax.ShapeDtypeStruct((bsz, npair, LANES, SSM_STATE), F32)],
        scratch_shapes=[pltpu.VMEM((SUBLANES, SSM_CONV_DIM), F32), pltpu.VMEM((npair, LANES, SSM_STATE), F32)],
        compiler_params=_cparams(("arbitrary", "arbitrary")),
        name="mamba",
    )(sm, conv0, h0, P["cw"], P["cb"], P["dtb"], P["alog"], P["dvec"], P["nrm"], tri, eye, xp, xpt,
      _block_ones(bw // SSM_GROUPS, bw // SSM_GROUPS))
    return o, h_t


def _topk_rows(s, k):
    return _topk_rows_many([s], k)[0]


def _topk_rows_many(arrays, k):
    arrays = list(arrays)
    iotas = [lax.broadcasted_iota(jnp.int32, s.shape, 0) for s in arrays]
    outs = [[] for _ in arrays]
    for _ in range(k):
        for a, s in enumerate(arrays):
            m = jnp.max(s, axis=0, keepdims=True)
            outs[a].append(m)
            idx = jnp.min(jnp.where(s == m, iotas[a], s.shape[0]), axis=0, keepdims=True)
            arrays[a] = jnp.where(iotas[a] == idx, -jnp.inf, s)
    return [jnp.concatenate(o, axis=0) for o in outs]


def _peer_route_kernel(q_ref, k1_ref, k2_ref, s1_ref, s2_ref, e1_ref, e2_ref, tau_ref):
    sc = []
    for half, (k_ref, s_ref) in enumerate(((k1_ref, s1_ref), (k2_ref, s2_ref))):
        kh, kl = _split2(k_ref[...])
        qh, ql = _split2(q_ref[:, half * PEER_HALF:(half + 1) * PEER_HALF])
        s = _dot_nt(kh, qh) + _dot_nt(kh, ql) + _dot_nt(kl, qh)
        s_ref[...] = s
        sc.append(s)
    v1, v2 = _topk_rows_many(sc, PEER_TOPK)
    row8 = lax.broadcasted_iota(jnp.int32, (SUBLANES, v2.shape[1]), 0)
    pieces = [v1[0:1, :] + v2]
    for i in range(1, SUBLANES):
        n_i = PEER_TOPK // (i + 1)
        pieces.append(jnp.where(row8 < n_i, v1[i:i + 1, :] + v2[0:SUBLANES, :], -jnp.inf))
    pieces.append(v1[SUBLANES:PEER_TOPK, :] + v2[0:1, :])
    cand = jnp.concatenate(pieces, axis=0)
    top = _topk_rows(cand, PEER_TOPK)
    tau = top[PEER_TOPK - 1:PEER_TOPK, :]
    zsum = jnp.sum(jnp.exp(top - top[0:1, :]), axis=0, keepdims=True)
    e1_ref[...] = jnp.exp(sc[0] - v1[0:1, :]) / zsum
    e2_ref[...] = jnp.exp(sc[1] - v2[0:1, :])
    tau_ref[...] = jnp.broadcast_to(tau, tau_ref.shape)


def _peer_route(q, k1, k2, tt):
    n = q.shape[0]
    arr = lambda: jax.ShapeDtypeStruct((PEER_HEADS, PEER_NKEYS, n), F32)
    spec = pl.BlockSpec((None, PEER_NKEYS, tt), lambda i, h: (h, 0, i))
    return pl.pallas_call(
        _peer_route_kernel,
        grid=(n // tt, PEER_HEADS),
        in_specs=[pl.BlockSpec((tt, PEER_KEY_DIM), lambda i, h: (i, h)),
                  pl.BlockSpec((PEER_NKEYS, PEER_HALF), lambda i, h: (0, 0)),
                  pl.BlockSpec((PEER_NKEYS, PEER_HALF), lambda i, h: (0, 0))],
        out_specs=[spec, spec, spec, spec, pl.BlockSpec((None, SUBLANES, tt), lambda i, h: (h, 0, i))],
        out_shape=[arr(), arr(), arr(), arr(), jax.ShapeDtypeStruct((PEER_HEADS, SUBLANES, n), F32)],
        compiler_params=_cparams(("arbitrary", "arbitrary")),
        name="peer_route",
    )(q, k1, k2)


def _peer_dense_kernel(ht_ref, u_ref, vt_ref, s1_ref, s2_ref, e1_ref, e2_ref, tau_ref, o_ref, act_scr, ca_scr,
                       *, ib, nsplit):
    j = pl.program_id(1)

    @pl.when(j == 0)
    def _():
        o_ref[...] = jnp.zeros(o_ref.shape, F32)

    ht = ht_ref[...]
    per = ib // nsplit
    rows = per * PEER_NKEYS
    for part in range(nsplit):
        act_scr[part * rows:(part + 1) * rows, :] = _dot(u_ref[part * rows:(part + 1) * rows, :], ht)
    for part in range(nsplit):
        for ii in range(part * per, (part + 1) * per):
            sl = slice(ii * PEER_NKEYS, (ii + 1) * PEER_NKEYS)
            act = act_scr[sl, :]
            act = 0.5 * act * (1.0 + lax.erf(act * (2.0 ** -0.5)))
            coef = jnp.zeros(act.shape, F32)
            for h in range(PEER_HEADS):
                s1 = s1_ref[h, ii:ii + 1, :]
                keep = (s1 + s2_ref[h]) >= tau_ref[h, 0:1, :]
                coef = coef + jnp.where(keep, e1_ref[h, ii:ii + 1, :] * e2_ref[h], 0.0)
            ca_scr[sl, :] = (coef * act).astype(BF16)
        o_ref[...] += _dot(vt_ref[:, part * rows:(part + 1) * rows], ca_scr[part * rows:(part + 1) * rows, :])


def _peer_dense(ht, u, vt, s1, s2, e1, e2, tau, tt, ib, nsplit=2):
    d, n = ht.shape
    eb = ib * PEER_NKEYS
    kern = functools.partial(_peer_dense_kernel, ib=ib, nsplit=nsplit)
    full = pl.BlockSpec((PEER_HEADS, PEER_NKEYS, tt), lambda i, j: (0, 0, i))
    part = pl.BlockSpec((PEER_HEADS, ib, tt), lambda i, j: (0, j, i))
    return pl.pallas_call(
        kern,
        grid=(n // tt, PEER_EXPERTS // eb),
        in_specs=[
            pl.BlockSpec((d, tt), lambda i, j: (0, i)),
            pl.BlockSpec((eb, d), lambda i, j: (j, 0)),
            pl.BlockSpec((d, eb), lambda i, j: (0, j)),
            part, full, part, full,
            pl.BlockSpec((PEER_HEADS, SUBLANES, tt), lambda i, j: (0, 0, i)),
        ],
        out_specs=pl.BlockSpec((d, tt), lambda i, j: (0, i)),
        out_shape=jax.ShapeDtypeStruct((d, n), F32),
        scratch_shapes=[pltpu.VMEM((eb, tt), F32), pltpu.VMEM((eb, tt), BF16)],
        compiler_params=_cparams(("arbitrary", "arbitrary")),
        name="peer_dense",
    )(ht, u, vt, s1, s2, e1, e2, tau)


def _peer_res_kernel(x_ref, ot_ref, g_ref, o_ref):
    o_ref[...] = x_ref[...] + g_ref[...] * ot_ref[...].T


def _peer_res(x, out_t, mod, g_blk, tm):
    n, d = x.shape
    nseq, r, _ = mod.shape
    tiles_per_seq = (n // tm) // nseq
    return pl.pallas_call(
        _peer_res_kernel,
        grid=(n // tm,),
        in_specs=[pl.BlockSpec((tm, d), lambda i: (i, 0)),
                  pl.BlockSpec((d, tm), lambda i: (0, i)),
                  pl.BlockSpec((None, r, d), lambda i: (i // tiles_per_seq, 0, g_blk))],
        out_specs=pl.BlockSpec((tm, d), lambda i: (i, 0)),
        out_shape=jax.ShapeDtypeStruct((n, d), F32),
        compiler_params=_cparams(("arbitrary",)),
        name="peer_res",
    )(x, out_t, mod)


def _pad_cols(a, width):
    return jnp.pad(a, [(0, 0)] * (a.ndim - 1) + [(0, width - a.shape[-1])])


def _pad_rows(a, rows):
    return jnp.pad(a, [(0, rows - a.shape[0])] + [(0, 0)] * (a.ndim - 1))


def _layer_weights(l, W):
    bw = BRANCH_WIDTH
    w_in = W["w_in"][l]
    out = {
        "w_qkv": w_in[:, :RWKV_OFF].astype(BF16),
        "w_rw": _pad_cols(w_in[:, RWKV_OFF:SSM_OFF], RWKV_COLS_PAD).astype(BF16),
        "w_sm": _pad_cols(w_in[:, SSM_OFF:GATE_OFF], SSM_COLS_PAD).astype(BF16),
        "w_gate": w_in[:, GATE_OFF:].astype(BF16),
        "wb": W["w_branch"][l].astype(BF16),
        "wo": W["w_o"][l].astype(BF16),
        "wq": W["peer_wq"][l].astype(BF16),
        "u": W["peer_u"][l].astype(BF16),
        "vt": W["peer_v"][l].T.astype(BF16),
    }
    o1 = 3 * bw
    o2 = o1 + RWKV_DECAY_LORA
    o3 = o2 + RWKV_AAA_LORA
    rwp = {
        "mu": _pad_cols(W["rwkv_mu"][l][None, :], RWKV_COLS_PAD),
        "w0": W["rwkv_w0"][l][None, :],
        "w2": _pad_rows(W["rwkv_w2"][l], LANES).astype(BF16),
        "a0": W["rwkv_a0"][l][None, :],
        "a2": jnp.concatenate([jnp.zeros((RWKV_DECAY_LORA, bw), F32), W["rwkv_a2"][l]], axis=0).astype(BF16),
        "g2": _pad_rows(W["rwkv_g2"][l], 2 * LANES).astype(BF16),
        "kk": W["rwkv_kk"][l][None, :],
        "ka": W["rwkv_ka"][l][None, :],
        "rk": W["rwkv_rk"][l].reshape(1, bw),
        "lng": W["rwkv_lnx_g"][l][None, :],
        "lnb": W["rwkv_lnx_b"][l][None, :],
    }
    smp = {
        "cw": W["ssm_conv_w"][l],
        "cb": W["ssm_conv_b"][l][None, :],
        "dtb": _pad_cols(W["ssm_dt_bias"][l][None, :], LANES),
        "alog": _pad_cols(W["ssm_a_log"][l][None, :], LANES),
        "dvec": jnp.repeat(W["ssm_d"][l], SSM_HEAD_DIM)[None, :],
        "nrm": W["ssm_norm"][l][None, :],
    }
    lv = W["att_lambda"][l].astype(F32)
    lam_init = 0.8 - 0.6 * math.exp(-0.3 * l)
    lam = jnp.exp(jnp.sum(lv[0] * lv[1])) - jnp.exp(jnp.sum(lv[2] * lv[3])) + lam_init
    out.update(rwp=rwp, smp=smp, lam=lam, lam_init=lam_init)
    return out


def _layer(l, x, mod, lw, W, bsz, t, state, cfg, paged):
    n = bsz * t
    tm, tn = cfg["tm"], cfg["tn"]
    shift0, wkv0, conv0, ssm0 = state
    mm = lambda w, tn_, dt=F32: _normmm(x, W["norm1"][l], mod, 1, 0, w, tm, tn_, dt)
    qkv = mm(lw["w_qkv"], 512)
    rw = mm(lw["w_rw"], 384)
    sm = mm(lw["w_sm"], 640)
    gates = mm(lw["w_gate"], 512, BF16)
    if paged is None:
        o_att = _attn_prompt(qkv, W["rel_bias"], lw["lam"], W["att_subln"][l], lw["lam_init"], bsz, t, cfg["tq"])
    else:
        cache_k, cache_v, page_table = paged
        o_att = _attn_sample(qkv, cache_k, cache_v, page_table, l, W["rel_bias"], lw["lam"], W["att_subln"][l],
                             lw["lam_init"], bsz, t)
    prep = _rwkv_prep(rw, shift0, lw["rwp"], bsz, t, cfg["rw_tm"])
    o_rwkv, wkv1 = _rwkv_scan(prep, wkv0, lw["rwp"]["lng"], lw["rwp"]["lnb"], bsz, t, cfg["rw_bb"], cfg["rw_tc"],
                               cfg["rw_pg"])
    o_ssm, ssm1 = _mamba(sm, conv0, ssm0, lw["smp"], bsz, t)
    merged = _merge(o_att, o_rwkv, o_ssm, gates, lw["wb"], cfg["tm2"], tn)
    x = _proj_res(merged, lw["wo"], x, mod, 2, cfg["tm2"], tn)
    npad = -(-n // LANES) * LANES
    xq, modq, tmq = x, mod, tm
    if npad != n:
        xq = _pad_rows(x, npad)
        modq = jnp.pad(mod, ((0, 0), (0, npad - n), (0, 0)))
        tmq = npad
    q = _normmm(xq, W["norm2"][l], modq, 4, 3, lw["wq"], tmq, tn)
    ht = _normT(xq, W["norm2"][l], modq, 4, 3, cfg["tmT"])
    s1, s2, e1, e2, tau = _peer_route(q, W["peer_k1"][l], W["peer_k2"][l], cfg["tt_route"])
    out_t = _peer_dense(ht, lw["u"], lw["vt"], s1, s2, e1, e2, tau, cfg["tt_dense"], cfg["ib"])
    x = _peer_res(xq, out_t, modq, 5, cfg["tmT"])[:n]
    qkv3 = qkv.reshape(bsz, t, 3 * ATT_QK)
    k_new = qkv3[:, :, ATT_QK:2 * ATT_QK].reshape(bsz, t, ATT_HEADS, 2 * ATT_DH)
    v_new = qkv3[:, :, 2 * ATT_QK:].reshape(bsz, t, ATT_HEADS, ATT_VD)
    shift1 = rw.reshape(bsz, t, RWKV_COLS_PAD)[:, -1, :RWKV_COLS]
    assert t >= SSM_CONV - 1
    conv1 = sm.reshape(bsz, t, SSM_COLS_PAD)[:, -(SSM_CONV - 1):, BRANCH_WIDTH:BRANCH_WIDTH + SSM_CONV_DIM]
    return x, (k_new, v_new, shift1, wkv1, conv1, ssm1)


def _wkv_to_kernel(s):
    b = s.shape[0]
    s = s.reshape(b, RWKV_HEADS // 2, 2, RWKV_DH, RWKV_DH)
    return jnp.transpose(s, (0, 1, 3, 2, 4)).reshape(b, RWKV_HEADS // 2, RWKV_DH, 2 * RWKV_DH)


def _wkv_from_kernel(s):
    b = s.shape[0]
    s = s.reshape(b, RWKV_HEADS // 2, RWKV_DH, 2, RWKV_DH)
    return jnp.transpose(s, (0, 1, 3, 2, 4)).reshape(b, RWKV_HEADS, RWKV_DH, RWKV_DH)


def _run_group(x, mod_of_layer, W, lws, bsz, t, states, cfg, paged_of_layer):
    xs = x.reshape(bsz * t, D_MODEL)
    outs = []
    for l in range(DEPTH):
        xs, new = _layer(l, xs, mod_of_layer(l), lws[l], W, bsz, t, states[l], cfg, paged_of_layer(l))
        outs.append(new)
    y = _final_norm(xs, W["norm_f"], cfg["tm_f"]).reshape(bsz, t, D_MODEL)
    k = jnp.stack([o[0] for o in outs], axis=1)
    v = jnp.stack([o[1] for o in outs], axis=1)
    shift = jnp.stack([o[2] for o in outs], axis=0)
    wkv = jnp.stack([_wkv_from_kernel(o[3]) for o in outs], axis=0)
    conv = jnp.stack([o[4] for o in outs], axis=0)
    ssm = jnp.stack([o[5].reshape(bsz, SSM_HEADS, SSM_HEAD_DIM, SSM_STATE) for o in outs], axis=0)
    return y, k, v, shift, wkv, conv, ssm


PROMPT_CFG = dict(tm=1024, tn=512, tm2=512, tq=512, rw_tm=256, rw_bb=2, rw_tc=128, rw_pg=4, tmT=256,
                  tt_route=256, tt_dense=512, ib=8, tm_f=512)


def _sample_cfg(n, t):
    npad = -(-n // LANES) * LANES
    return dict(tm=n, tn=512, tm2=n, tq=None, rw_tm=t, rw_bb=2, rw_tc=t, rw_pg=4, tmT=npad,
                tt_route=npad, tt_dense=npad, ib=8, tm_f=n)


def kernel(x_prompt, x_sample, cache_k, cache_v, state_rwkv_shift, state_rwkv, state_conv, state_ssm, page_table, c_prompt, c_sample, w_ada, b_ada, norm1, norm2, norm_f, w_in, rel_bias, att_lambda, att_subln, rwkv_mu, rwkv_w0, rwkv_w2, rwkv_a0, rwkv_a2, rwkv_g2, rwkv_kk, rwkv_ka, rwkv_rk, rwkv_lnx_g, rwkv_lnx_b, ssm_conv_w, ssm_conv_b, ssm_dt_bias, ssm_a_log, ssm_d, ssm_norm, w_branch, w_o, peer_wq, peer_k1, peer_k2, peer_u, peer_v):
    W = dict(norm1=norm1, norm2=norm2, norm_f=norm_f, w_in=w_in, rel_bias=rel_bias, att_lambda=att_lambda,
             att_subln=att_subln, rwkv_mu=rwkv_mu, rwkv_w0=rwkv_w0, rwkv_w2=rwkv_w2, rwkv_a0=rwkv_a0,
             rwkv_a2=rwkv_a2, rwkv_g2=rwkv_g2, rwkv_kk=rwkv_kk, rwkv_ka=rwkv_ka, rwkv_rk=rwkv_rk,
             rwkv_lnx_g=rwkv_lnx_g, rwkv_lnx_b=rwkv_lnx_b, ssm_conv_w=ssm_conv_w, ssm_conv_b=ssm_conv_b,
             ssm_dt_bias=ssm_dt_bias, ssm_a_log=ssm_a_log, ssm_d=ssm_d, ssm_norm=ssm_norm, w_branch=w_branch,
             w_o=w_o, peer_wq=peer_wq, peer_k1=peer_k1, peer_k2=peer_k2, peer_u=peer_u, peer_v=peer_v)
    bp, tp, _ = x_prompt.shape
    bs, ts, _ = x_sample.shape
    nseq = bp + bs
    rows = -(-nseq // SUBLANES) * SUBLANES
    c_all = _pad_rows(jnp.concatenate([c_prompt, c_sample], axis=0), rows)
    mod = _ada(c_all, w_ada, b_ada)
    lws = [_layer_weights(l, W) for l in range(DEPTH)]
    npair = RWKV_HEADS // 2
    zero_state = (jnp.zeros((bp, RWKV_COLS_PAD), F32),
                  jnp.zeros((bp, npair, RWKV_DH, LANES), F32),
                  jnp.zeros((bp, SUBLANES, SSM_CONV_DIM), F32),
                  jnp.zeros((bp, SSM_HEADS // 2, LANES, SSM_STATE), F32))
    outs_p = _run_group(x_prompt, lambda l: mod[l, :bp].reshape(bp, 1, 6 * D_MODEL), W, lws, bp, tp,
                        [zero_state] * DEPTH, PROMPT_CFG, lambda l: None)
    states_s = []
    for l in range(DEPTH):
        states_s.append((
            _pad_cols(state_rwkv_shift[l], RWKV_COLS_PAD),
            _wkv_to_kernel(state_rwkv[l]),
            jnp.pad(state_conv[l], ((0, 0), (SUBLANES - (SSM_CONV - 1), 0), (0, 0))),
            state_ssm[l].reshape(bs, SSM_HEADS // 2, LANES, SSM_STATE),
        ))
    ns = bs * ts
    mod_s = lambda l: jnp.repeat(mod[l, bp:bp + bs], ts, axis=0).reshape(1, ns, 6 * D_MODEL)
    outs_s = _run_group(x_sample, mod_s, W, lws, bs, ts, states_s, _sample_cfg(ns, ts),
                        lambda l: (cache_k, cache_v, page_table))
    return (outs_p[0], outs_s[0], *outs_p[1:], *outs_s[1:])
```

```python
import functools
import math

import jax
import jax.numpy as jnp
import numpy as np
from jax import lax
from jax.experimental import pallas as pl
from jax.experimental.pallas import tpu as pltpu

F32 = jnp.float32
BF16 = jnp.bfloat16

LANES = 128
SUBLANES = 8
VMEM_LIMIT_BYTES = 56 * 1024 * 1024

D_MODEL = 2048
DEPTH = 4
PAGE_SIZE = 128
BRANCH_WIDTH = D_MODEL // 2
N_BRANCHES = 3
ATT_DH = 64
ATT_VD = 2 * ATT_DH
ATT_HEADS = BRANCH_WIDTH // ATT_VD
ATT_QK = ATT_HEADS * 2 * ATT_DH
REL_BUCKETS = 32
REL_MAX_DIST = 128
RWKV_DH = 64
RWKV_HEADS = BRANCH_WIDTH // RWKV_DH
RWKV_DECAY_LORA = 64
RWKV_AAA_LORA = 64
RWKV_GATE_LORA = 160
RWKV_LN_EPS = 64e-5
RWKV_COLS = 3 * BRANCH_WIDTH + RWKV_DECAY_LORA + RWKV_AAA_LORA + RWKV_GATE_LORA
RWKV_COLS_PAD = 3456
SSM_HEAD_DIM = 64
SSM_HEADS = BRANCH_WIDTH // SSM_HEAD_DIM
SSM_GROUPS = 4
SSM_STATE = 128
SSM_CONV = 4
SSM_CONV_DIM = BRANCH_WIDTH + 2 * SSM_GROUPS * SSM_STATE
SSM_CHUNK = 128
SSM_COLS = BRANCH_WIDTH + SSM_CONV_DIM + SSM_HEADS
SSM_COLS_PAD = 3200
PEER_HEADS = 8
PEER_NKEYS = 128
PEER_EXPERTS = PEER_NKEYS * PEER_NKEYS
PEER_KEY_DIM = 256
PEER_HALF = PEER_KEY_DIM // 2
PEER_TOPK = 16
ATT_Q_OFF = 0
ATT_K_OFF = ATT_QK
ATT_V_OFF = 2 * ATT_QK
RWKV_OFF = ATT_V_OFF + ATT_HEADS * ATT_VD
SSM_OFF = RWKV_OFF + RWKV_COLS
GATE_OFF = SSM_OFF + SSM_COLS

NEG_BIG = -1e30


def _cparams(semantics):
    return pltpu.CompilerParams(dimension_semantics=semantics, vmem_limit_bytes=VMEM_LIMIT_BYTES)


def _dot(a, b):
    return jnp.dot(a, b, preferred_element_type=F32)


def _dot_nt(a, b):
    return lax.dot_general(a, b, (((1,), (1,)), ((), ())), preferred_element_type=F32)


def _dot_tn(a, b):
    return lax.dot_general(a, b, (((0,), (0,)), ((), ())), preferred_element_type=F32)


def _split2(x):
    hi = x.astype(BF16)
    lo = (x - hi.astype(F32)).astype(BF16)
    return hi, lo


def _split3(x):
    hi = x.astype(BF16)
    r = x - hi.astype(F32)
    mid = r.astype(BF16)
    lo = (r - mid.astype(F32)).astype(BF16)
    return hi, mid, lo


def _dot_x3(x, w_bf16):
    hi, mid, lo = _split3(x)
    return _dot(hi, w_bf16) + _dot(mid, w_bf16) + _dot(lo, w_bf16)


def _dot_x2(x, w_bf16):
    hi, lo = _split2(x)
    return _dot(hi, w_bf16) + _dot(lo, w_bf16)


def _block_ones(n, width):
    r = np.arange(n) // width
    return jnp.asarray((r[:, None] == r[None, :]).astype(np.float32), dtype=BF16)


def _segsum_bcast(x, ones_ref, width_block):
    cols = x.shape[1]
    outs = []
    e = ones_ref[...]
    for c in range(cols // width_block):
        outs.append(_dot_x2(x[:, c * width_block:(c + 1) * width_block], e))
    return outs[0] if len(outs) == 1 else jnp.concatenate(outs, axis=1)


def _silu(x):
    return x * jax.nn.sigmoid(x)


def _softplus(x):
    return jnp.maximum(x, 0.0) + jnp.log1p(jnp.exp(-jnp.abs(x)))


def _ada_kernel(c_ref, w_ref, b_ref, o_ref):
    c = c_ref[...]
    a = _silu(c).astype(BF16)
    o_ref[...] = _dot(a, w_ref[...].astype(BF16)) + b_ref[...]


def _ada(c_all, w_ada, b_ada):
    rows = c_all.shape[0]
    tn = 1024
    ncol = w_ada.shape[2]
    return pl.pallas_call(
        _ada_kernel,
        grid=(DEPTH, ncol // tn),
        in_specs=[
            pl.BlockSpec((rows, D_MODEL), lambda l, j: (0, 0)),
            pl.BlockSpec((None, D_MODEL, tn), lambda l, j: (l, 0, j)),
            pl.BlockSpec((None, 1, tn), lambda l, j: (l, 0, j)),
        ],
        out_specs=pl.BlockSpec((None, rows, tn), lambda l, j: (l, 0, j)),
        out_shape=jax.ShapeDtypeStruct((DEPTH, rows, ncol), F32),
        compiler_params=_cparams(("arbitrary", "arbitrary")),
        name="ada",
    )(c_all, w_ada, b_ada.reshape(DEPTH, 1, ncol))


def _norm_rows(x, g, eps):
    return x * lax.rsqrt(jnp.mean(x * x, axis=-1, keepdims=True) + eps) * g


def _normmm_kernel(x_ref, g_ref, sc_ref, sh_ref, w_ref, o_ref, h_scr):
    @pl.when(pl.program_id(1) == 0)
    def _():
        h = _norm_rows(x_ref[...], g_ref[...], 1e-6) * (1.0 + sc_ref[...]) + sh_ref[...]
        h_scr[...] = h.astype(BF16)

    o_ref[...] = _dot(h_scr[...], w_ref[...]).astype(o_ref.dtype)


def _normmm(x, g, mod, sc_blk, sh_blk, w, l, tm, tn, out_dtype=F32):
    n, d = x.shape
    ncol = w.shape[2]
    nseq, r, _ = mod.shape
    tiles_per_seq = (n // tm) // nseq
    return pl.pallas_call(
        _normmm_kernel,
        grid=(n // tm, ncol // tn),
        in_specs=[
            pl.BlockSpec((tm, d), lambda i, j: (i, 0)),
            pl.BlockSpec((1, d), lambda i, j: (0, 0)),
            pl.BlockSpec((None, r, d), lambda i, j: (i // tiles_per_seq, 0, sc_blk)),
            pl.BlockSpec((None, r, d), lambda i, j: (i // tiles_per_seq, 0, sh_blk)),
            pl.BlockSpec((None, d, tn), lambda i, j: (l, 0, j)),
        ],
        out_specs=pl.BlockSpec((tm, tn), lambda i, j: (i, j)),
        out_shape=jax.ShapeDtypeStruct((n, ncol), out_dtype),
        scratch_shapes=[pltpu.VMEM((tm, d), BF16)],
        compiler_params=_cparams(("arbitrary", "arbitrary")),
        name="normmm",
    )(x, g.reshape(1, d), mod, mod, w)


def _normT_kernel(x_ref, g_ref, sc_ref, sh_ref, o_ref):
    h = _norm_rows(x_ref[...], g_ref[...], 1e-6) * (1.0 + sc_ref[...]) + sh_ref[...]
    o_ref[...] = h.T.astype(BF16)


def _normT(x, g, mod, sc_blk, sh_blk, tm):
    n, d = x.shape
    nseq, r, _ = mod.shape
    tiles_per_seq = (n // tm) // nseq
    return pl.pallas_call(
        _normT_kernel,
        grid=(n // tm,),
        in_specs=[
            pl.BlockSpec((tm, d), lambda i: (i, 0)),
            pl.BlockSpec((1, d), lambda i: (0, 0)),
            pl.BlockSpec((None, r, d), lambda i: (i // tiles_per_seq, 0, sc_blk)),
            pl.BlockSpec((None, r, d), lambda i: (i // tiles_per_seq, 0, sh_blk)),
        ],
        out_specs=pl.BlockSpec((d, tm), lambda i: (0, i)),
        out_shape=jax.ShapeDtypeStruct((d, n), BF16),
        compiler_params=_cparams(("arbitrary",)),
        name="normT",
    )(x, g.reshape(1, d), mod, mod)


def _final_norm_kernel(x_ref, g_ref, o_ref):
    o_ref[...] = _norm_rows(x_ref[...], g_ref[...], 1e-6)


def _final_norm(x, g, tm):
    n, d = x.shape
    return pl.pallas_call(
        _final_norm_kernel,
        grid=(n // tm,),
        in_specs=[pl.BlockSpec((tm, d), lambda i: (i, 0)), pl.BlockSpec((1, d), lambda i: (0, 0))],
        out_specs=pl.BlockSpec((tm, d), lambda i: (i, 0)),
        out_shape=jax.ShapeDtypeStruct((n, d), F32),
        compiler_params=_cparams(("arbitrary",)),
        name="final_norm",
    )(x, g.reshape(1, d))


def _merge_kernel(o0_ref, o1_ref, o2_ref, g0_ref, g1_ref, g2_ref, w_ref, out_ref):
    acc = jax.nn.sigmoid(g0_ref[...].astype(F32)) * _dot(o0_ref[...], w_ref[0])
    acc += jax.nn.sigmoid(g1_ref[...].astype(F32)) * _dot(o1_ref[...], w_ref[1])
    acc += jax.nn.sigmoid(g2_ref[...].astype(F32)) * _dot(o2_ref[...], w_ref[2])
    out_ref[...] = acc.astype(out_ref.dtype)


def _merge(o_att, o_rwkv, o_ssm, gates, wb, l, tm, tn):
    n = o_att.shape[0]
    gblk = D_MODEL // tn
    o_spec = pl.BlockSpec((tm, BRANCH_WIDTH), lambda i, j: (i, 0))
    g_specs = [pl.BlockSpec((tm, tn), functools.partial(lambda i, j, b: (i, b * gblk + j), b=b)) for b in range(3)]
    return pl.pallas_call(
        _merge_kernel,
        grid=(n // tm, D_MODEL // tn),
        in_specs=[o_spec, o_spec, o_spec, *g_specs,
                  pl.BlockSpec((None, N_BRANCHES, BRANCH_WIDTH, tn), lambda i, j: (l, 0, 0, j))],
        out_specs=pl.BlockSpec((tm, tn), lambda i, j: (i, j)),
        out_shape=jax.ShapeDtypeStruct((n, D_MODEL), BF16),
        compiler_params=_cparams(("arbitrary", "arbitrary")),
        name="merge",
    )(o_att, o_rwkv, o_ssm, gates, gates, gates, wb)


def _proj_res_kernel(a_ref, w_ref, x_ref, g_ref, o_ref):
    o_ref[...] = x_ref[...] + g_ref[...] * _dot(a_ref[...], w_ref[...])


def _proj_res(a, w, l, x, mod, g_blk, tm, tn):
    n, k = a.shape
    nseq, r, _ = mod.shape
    tiles_per_seq = (n // tm) // nseq
    nblk = D_MODEL // tn
    return pl.pallas_call(
        _proj_res_kernel,
        grid=(n // tm, D_MODEL // tn),
        in_specs=[
            pl.BlockSpec((tm, k), lambda i, j: (i, 0)),
            pl.BlockSpec((None, k, tn), lambda i, j: (l, 0, j)),
            pl.BlockSpec((tm, tn), lambda i, j: (i, j)),
            pl.BlockSpec((None, r, tn), lambda i, j: (i // tiles_per_seq, 0, g_blk * nblk + j)),
        ],
        out_specs=pl.BlockSpec((tm, tn), lambda i, j: (i, j)),
        out_shape=jax.ShapeDtypeStruct((n, D_MODEL), F32),
        compiler_params=_cparams(("arbitrary", "arbitrary")),
        name="proj_res",
    )(a, w, x, mod)


def _rel_bucket(dist):
    n = jnp.maximum(dist, 0)
    max_exact = REL_BUCKETS // 2
    nf = jnp.maximum(n, 1).astype(F32)
    large = max_exact + (jnp.log(nf / max_exact) / math.log(REL_MAX_DIST / max_exact)
                         * (REL_BUCKETS - max_exact)).astype(jnp.int32)
    large = jnp.minimum(large, REL_BUCKETS - 1)
    return jnp.where(n < max_exact, n, large)


def _bias_lookup(rel_bias, dist):
    onehot = (_rel_bucket(dist)[..., None] == jnp.arange(REL_BUCKETS, dtype=jnp.int32)).astype(F32)
    return jnp.einsum("...b,bh->...h", onehot, rel_bias.astype(F32), precision=lax.Precision.HIGHEST)


def _bias_of_dist(rel_bias, dist):
    b = jnp.moveaxis(_bias_lookup(rel_bias, dist), -1, 0)
    return jnp.where(dist[None] >= 0, b, NEG_BIG)


def _softmax_step(s, v_bf16, m_scr, l_scr, acc_scr):
    width = min(LANES, s.shape[1])
    cols = [s[:, c:c + width] for c in range(0, s.shape[1], width)]
    m_prev = m_scr[...]
    m_new = jnp.maximum(m_prev, jnp.max(functools.reduce(jnp.maximum, cols), axis=-1, keepdims=True))
    alpha = jnp.exp(m_prev - m_new)
    ps = [jnp.exp(c - m_new[:, :width]) for c in cols]
    l_scr[...] = alpha * l_scr[...] + jnp.sum(functools.reduce(jnp.add, ps), axis=-1, keepdims=True)
    p = ps[0] if len(ps) == 1 else jnp.concatenate(ps, axis=1)
    acc_scr[...] = alpha * acc_scr[...] + _dot(p.astype(BF16), v_bf16)
    m_scr[...] = m_new


def _two_map_queries(q, scale):
    lane = lax.broadcasted_iota(jnp.int32, q.shape, 1)
    qs = q * scale
    q0 = jnp.where(lane < ATT_DH, qs, 0.0)
    q1 = jnp.where(lane >= ATT_DH, qs, 0.0)
    return jnp.concatenate([q0, q1], axis=0).astype(BF16)


def _diff_finalize(acc, l, lam, g, rows, post_scale):
    o0 = acc[:rows] / l[:rows]
    o1 = acc[rows:] / l[rows:]
    o = o0 - lam * o1
    o = o * lax.rsqrt(jnp.mean(o * o, axis=-1, keepdims=True) + 1e-5) * g
    return o * post_scale


def _attn_prompt_kernel(qi_ref, ki_ref, lam_ref, q_ref, k_ref, v_ref, bias_ref, g_ref, o_ref,
                        m_scr, l_scr, acc_scr, *, tq, post_scale):
    p = pl.program_id(2)
    qi = qi_ref[p]
    ki = ki_ref[p]

    @pl.when(ki == 0)
    def _():
        m_scr[...] = jnp.full(m_scr.shape, NEG_BIG, F32)
        l_scr[...] = jnp.zeros(l_scr.shape, F32)
        acc_scr[...] = jnp.zeros(acc_scr.shape, F32)

    qq = _two_map_queries(q_ref[...], ATT_DH ** -0.5)
    s = _dot_nt(qq, k_ref[...].astype(BF16))
    b = bias_ref[...]
    s = s + jnp.concatenate([b, b], axis=0)
    _softmax_step(s, v_ref[...].astype(BF16), m_scr, l_scr, acc_scr)

    @pl.when(ki == qi)
    def _():
        o = _diff_finalize(acc_scr[...], l_scr[...], lam_ref[0, 0], g_ref[...], tq, post_scale)
        o_ref[...] = o.astype(o_ref.dtype)


def _attn_prompt(qkv, rel_bias, lam, subln, lam_init, bsz, t, tq):
    nq = t // tq
    pairs = [(a, b) for a in range(nq) for b in range(a + 1)]
    qi_tab = jnp.asarray([a for a, _ in pairs], jnp.int32)
    ki_tab = jnp.asarray([b for _, b in pairs], jnp.int32)
    assert tq % REL_MAX_DIST == 0
    nblk = tq // REL_MAX_DIST
    ii = jnp.arange(REL_MAX_DIST, dtype=jnp.int32)
    d0 = ii[:, None] - ii[None, :]
    blocks = [_bias_of_dist(rel_bias, d0 + k * REL_MAX_DIST) for k in range(3)]
    masked = jnp.full_like(blocks[0], NEG_BIG)

    def tile(tile_dist):
        rows = []
        for bi in range(nblk):
            row = []
            for bj in range(nblk):
                dblk = tile_dist * nblk + bi - bj
                row.append(masked if dblk < 0 else blocks[min(dblk, 2)])
            rows.append(jnp.concatenate(row, axis=2))
        return jnp.concatenate(rows, axis=1)

    tiles = jnp.stack([tile(0), tile(1), tile(2)], axis=1)
    qkv3 = qkv.reshape(bsz, t, 3 * ATT_QK)
    kern = functools.partial(_attn_prompt_kernel, tq=tq, post_scale=1.0 - lam_init)
    grid_spec = pltpu.PrefetchScalarGridSpec(
        num_scalar_prefetch=2,
        grid=(bsz, ATT_HEADS, len(pairs)),
        in_specs=[
            pl.BlockSpec(memory_space=pltpu.SMEM),
            pl.BlockSpec((None, tq, LANES), lambda b, h, p, qt, kt: (b, qt[p], h)),
            pl.BlockSpec((None, tq, LANES), lambda b, h, p, qt, kt: (b, kt[p], ATT_HEADS + h)),
            pl.BlockSpec((None, tq, LANES), lambda b, h, p, qt, kt: (b, kt[p], 2 * ATT_HEADS + h)),
            pl.BlockSpec((None, None, tq, tq), lambda b, h, p, qt, kt: (h, jnp.minimum(qt[p] - kt[p], 2), 0, 0)),
            pl.BlockSpec((1, LANES), lambda b, h, p, qt, kt: (0, 0)),
        ],
        out_specs=pl.BlockSpec((None, tq, LANES), lambda b, h, p, qt, kt: (b, qt[p], h)),
        scratch_shapes=[pltpu.VMEM((2 * tq, LANES), F32), pltpu.VMEM((2 * tq, LANES), F32),
                        pltpu.VMEM((2 * tq, LANES), F32)],
    )
    out = pl.pallas_call(
        kern,
        grid_spec=grid_spec,
        out_shape=jax.ShapeDtypeStruct((bsz, t, BRANCH_WIDTH), BF16),
        compiler_params=_cparams(("arbitrary", "arbitrary", "arbitrary")),
        name="attn_prompt",
    )(qi_tab, ki_tab, lam.reshape(1, 1), qkv3, qkv3, qkv3, tiles, subln.reshape(1, ATT_VD))
    return out.reshape(bsz * t, BRANCH_WIDTH)


def _attn_sample_kernel(pt_ref, lam_ref, q_ref, kn_ref, vn_ref, bias_ref, biasn_ref, g_ref, *rest,
                        pps, rows, post_scale):
    kp_refs = rest[:pps]
    vp_refs = rest[pps:2 * pps]
    o_ref, m_scr, l_scr, acc_scr = rest[2 * pps:]
    s_idx = pl.program_id(1)
    last = s_idx == pl.num_programs(1) - 1

    @pl.when(s_idx == 0)
    def _():
        m_scr[...] = jnp.full(m_scr.shape, NEG_BIG, F32)
        l_scr[...] = jnp.zeros(l_scr.shape, F32)
        acc_scr[...] = jnp.zeros(acc_scr.shape, F32)

    qq = _two_map_queries(q_ref[...], ATT_DH ** -0.5)
    scores = []
    values = []
    for r in range(pps):
        kb = kp_refs[r][...].reshape(PAGE_SIZE * ATT_HEADS, LANES).astype(BF16)
        values.append(vp_refs[r][...].reshape(PAGE_SIZE * ATT_HEADS, LANES).astype(BF16))
        s = _dot_nt(qq, kb)
        if r == pps - 1:
            b = jnp.where(last, bias_ref[1], bias_ref[0])
        else:
            b = bias_ref[0]
        scores.append(s + jnp.concatenate([b, b], axis=0))
    _softmax_step(jnp.concatenate(scores, axis=1), jnp.concatenate(values, axis=0), m_scr, l_scr, acc_scr)

    @pl.when(last)
    def _():
        s = _dot_nt(qq, kn_ref[...].astype(BF16))
        b = biasn_ref[...]
        s = s + jnp.concatenate([b, b], axis=0)
        _softmax_step(s, vn_ref[...].astype(BF16), m_scr, l_scr, acc_scr)
        o = _diff_finalize(acc_scr[...], l_scr[...], lam_ref[0, 0], g_ref[...], rows, post_scale)
        o_ref[...] = o.astype(o_ref.dtype)


def _attn_sample(qkv, cache_k, cache_v, page_table, layer, rel_bias, lam, subln, lam_init, bsz, t):
    n_pages = page_table.shape[1]
    past = n_pages * PAGE_SIZE
    pps = 8
    assert n_pages % pps == 0
    rows = t * ATT_HEADS
    q = qkv[:, :ATT_QK].reshape(bsz, rows, LANES)
    kn = qkv[:, ATT_QK:2 * ATT_QK].reshape(bsz, rows, LANES)
    vn = qkv[:, 2 * ATT_QK:].reshape(bsz, rows, LANES)
    tok = jnp.repeat(jnp.arange(t, dtype=jnp.int32), ATT_HEADS)
    hq = jnp.tile(jnp.arange(ATT_HEADS, dtype=jnp.int32), t)
    kk = jnp.repeat(jnp.arange(PAGE_SIZE, dtype=jnp.int32), ATT_HEADS)
    hk = jnp.tile(jnp.arange(ATT_HEADS, dtype=jnp.int32), PAGE_SIZE)

    head_onehot = (hq[:, None] == jnp.arange(ATT_HEADS, dtype=jnp.int32)).astype(F32)

    def tile(dist, hk_):
        b = jnp.sum(_bias_lookup(rel_bias, dist) * head_onehot[:, None, :], axis=-1)
        return jnp.where((dist >= 0) & (hq[:, None] == hk_[None, :]), b, NEG_BIG)

    assert PAGE_SIZE >= REL_MAX_DIST
    far = tile(jnp.broadcast_to(jnp.int32(2 * PAGE_SIZE), (rows, PAGE_SIZE * ATT_HEADS)) + tok[:, None] - kk[None, :], hk)
    near = tile(PAGE_SIZE + tok[:, None] - kk[None, :], hk)
    bias_pages = jnp.stack([far, near])
    bias_new = tile(tok[:, None] - tok[None, :], hq)
    kern = functools.partial(_attn_sample_kernel, pps=pps, rows=rows, post_scale=1.0 - lam_init)
    page_block = (None, None, PAGE_SIZE, ATT_HEADS, LANES)
    page_specs = [pl.BlockSpec(page_block, functools.partial(lambda b, s, pt, r: (pt[b, s * pps + r], layer, 0, 0, 0), r=r))
                  for r in range(pps)]
    grid_spec = pltpu.PrefetchScalarGridSpec(
        num_scalar_prefetch=1,
        grid=(bsz, n_pages // pps),
        in_specs=[
            pl.BlockSpec(memory_space=pltpu.SMEM),
            pl.BlockSpec((None, rows, LANES), lambda b, s, pt: (b, 0, 0)),
            pl.BlockSpec((None, rows, LANES), lambda b, s, pt: (b, 0, 0)),
            pl.BlockSpec((None, rows, LANES), lambda b, s, pt: (b, 0, 0)),
            pl.BlockSpec((2, rows, PAGE_SIZE * ATT_HEADS), lambda b, s, pt: (0, 0, 0)),
            pl.BlockSpec((rows, rows), lambda b, s, pt: (0, 0)),
            pl.BlockSpec((1, LANES), lambda b, s, pt: (0, 0)),
            *page_specs, *page_specs,
        ],
        out_specs=pl.BlockSpec((None, rows, LANES), lambda b, s, pt: (b, 0, 0)),
        scratch_shapes=[pltpu.VMEM((2 * rows, LANES), F32), pltpu.VMEM((2 * rows, LANES), F32),
                        pltpu.VMEM((2 * rows, LANES), F32)],
    )
    out = pl.pallas_call(
        kern,
        grid_spec=grid_spec,
        out_shape=jax.ShapeDtypeStruct((bsz, rows, LANES), BF16),
        compiler_params=_cparams(("arbitrary", "arbitrary")),
        name="attn_sample",
    )(page_table, lam.reshape(1, 1), q, kn, vn, bias_pages, bias_new, subln.reshape(1, ATT_VD),
      *([cache_k] * pps), *([cache_v] * pps))
    return out.reshape(bsz * t, BRANCH_WIDTH)


RW_LORA_OFF = 3 * BRANCH_WIDTH
RW_G_OFF = RW_LORA_OFF + LANES


def _rwkv_prep_kernel(x_ref, prev_ref, shift_ref, mu_ref, w0_ref, w2_ref, a0_ref, a2_ref, g2_ref,
                      kkw_ref, ka_ref, rk_ref, e_ref,
                      a_out, wr_out, w_out, b_out, k_out, br_out, kr_out, v_out, bon_out, g_out):
    bw = BRANCH_WIDTH
    x = x_ref[...]
    tm = x.shape[0]
    first = pl.program_id(1) == 0
    prev_row = jnp.where(first, shift_ref[...], prev_ref[SUBLANES - 1:SUBLANES, :])
    row = lax.broadcasted_iota(jnp.int32, x.shape, 0)
    prev = jnp.where(row == 0, prev_row, pltpu.roll(x, 1, axis=0))
    pm = x + (prev - x) * mu_ref[...]
    r = pm[:, :bw]
    k = pm[:, bw:2 * bw]
    v = pm[:, 2 * bw:3 * bw]
    lora = pm[:, RW_LORA_OFF:RW_LORA_OFF + LANES]
    glo = pm[:, RW_G_OFF:RW_G_OFF + 2 * LANES]
    w = -_softplus(-(w0_ref[...] + _dot(jnp.tanh(lora).astype(BF16), w2_ref[...]))) - 0.5
    decay = jnp.exp(-jnp.exp(w))
    a = jax.nn.sigmoid(a0_ref[...] + _dot(lora.astype(BF16), a2_ref[...]))
    g = _dot(jax.nn.sigmoid(glo).astype(BF16), g2_ref[...])
    kk = k * kkw_ref[...]
    n2 = _segsum_bcast(kk * kk, e_ref, LANES)
    kk = kk / jnp.maximum(jnp.sqrt(n2), 1e-12)
    k2 = k * (1.0 + (a - 1.0) * ka_ref[...])
    bm = kk * a
    a_out[...] = -kk
    wr_out[...] = decay * r
    w_out[...] = decay
    b_out[...] = bm
    k_out[...] = k2
    br_out[...] = _segsum_bcast(bm * r, e_ref, LANES)
    kr_out[...] = _segsum_bcast(k2 * r, e_ref, LANES)
    v_out[...] = v
    bon_out[...] = _segsum_bcast(r * k2 * rk_ref[...], e_ref, LANES) * v
    g_out[...] = g


def _rwkv_prep(rw, shift0, P, bsz, t, tm):
    n = rw.shape[0]
    bw = BRANCH_WIDTH
    cp = RWKV_COLS_PAD
    tiles = t // tm
    vec = lambda nm: pl.BlockSpec((1, bw), lambda b, i: (0, 0))
    out_spec = pl.BlockSpec((tm, bw), lambda b, i: (b * tiles + i, 0))
    rows8 = tm // SUBLANES
    outs = pl.pallas_call(
        _rwkv_prep_kernel,
        grid=(bsz, tiles),
        in_specs=[
            pl.BlockSpec((tm, cp), lambda b, i: (b * tiles + i, 0)),
            pl.BlockSpec((SUBLANES, cp), lambda b, i: (jnp.maximum((b * tiles + i) * rows8 - 1, 0), 0)),
            pl.BlockSpec((None, 1, cp), lambda b, i: (b, 0, 0)),
            pl.BlockSpec((1, cp), lambda b, i: (0, 0)),
            vec("w0"),
            pl.BlockSpec((LANES, bw), lambda b, i: (0, 0)),
            vec("a0"),
            pl.BlockSpec((LANES, bw), lambda b, i: (0, 0)),
            pl.BlockSpec((2 * LANES, bw), lambda b, i: (0, 0)),
            vec("kk"), vec("ka"), vec("rk"),
            pl.BlockSpec((LANES, LANES), lambda b, i: (0, 0)),
        ],
        out_specs=[out_spec] * 10,
        out_shape=[jax.ShapeDtypeStruct((n, bw), F32)] * 10,
        compiler_params=_cparams(("arbitrary", "arbitrary")),
        name="rwkv_prep",
    )(rw, rw, shift0.reshape(bsz, 1, cp), P["mu"], P["w0"], P["w2"], P["a0"], P["a2"], P["g2"],
      P["kk"], P["ka"], P["rk"], _block_ones(LANES, RWKV_DH))
    return outs


def _rwkv_scan_kernel(a_ref, wr_ref, w_ref, b_ref, k_ref, br_ref, kr_ref, v_ref, bon_ref, g_ref,
                      s0_ref, lng_ref, lnb_ref, e_ref, o_ref, sT_ref, s_scr, y_scr, *, bb, tc, pg):
    c = pl.program_id(2)

    @pl.when(c == 0)
    def _():
        s_scr[...] = s0_ref[...]

    e = e_ref[...]
    sub = lax.broadcasted_iota(jnp.int32, (RWKV_DH, LANES), 0)
    lane = lax.broadcasted_iota(jnp.int32, (RWKV_DH, LANES), 1)
    diag = (lane % RWKV_DH) == sub

    chains = [(b, p) for b in range(bb) for p in range(pg)]

    def group(tg, carry):
        t0 = pl.multiple_of(tg * SUBLANES, SUBLANES)
        states = [s_scr[b, p] for b, p in chains]
        ys = [[] for _ in chains]
        for r in range(SUBLANES):
            for ci, (b, p) in enumerate(chains):
                cs = pl.ds(p * LANES, LANES)
                row = lambda ref: ref[b, pl.ds(t0, SUBLANES), cs][r:r + 1, :]
                s = states[ci]
                p1 = s * row(a_ref)
                p2 = s * row(wr_ref)
                z = jnp.where(diag, row(v_ref), 0.0)
                p1h, p1l = _split2(p1)
                lhs = jnp.concatenate([p1h, p1l, p2.astype(BF16), z.astype(BF16)], axis=0)
                res = _dot(lhs, e)
                sa = res[0:RWKV_DH] + res[RWKV_DH:2 * RWKV_DH]
                m2 = res[2 * RWKV_DH:3 * RWKV_DH]
                vb = res[3 * RWKV_DH:4 * RWKV_DH]
                states[ci] = s * row(w_ref) + sa * row(b_ref) + vb * row(k_ref)
                yb = m2 + sa * row(br_ref) + vb * row(kr_ref)
                ys[ci].append(jnp.sum(jnp.where(diag, yb, 0.0), axis=0, keepdims=True))
        for ci, (b, p) in enumerate(chains):
            s_scr[b, p] = states[ci]
            y_scr[b, pl.ds(t0, SUBLANES), pl.ds(p * LANES, LANES)] = jnp.concatenate(ys[ci], axis=0)
        return carry

    lax.fori_loop(0, tc // SUBLANES, group, 0)

    for b in range(bb):
        y = y_scr[b]
        mu = _segsum_bcast(y, e_ref, LANES) * (1.0 / RWKV_DH)
        d = y - mu
        var = _segsum_bcast(d * d, e_ref, LANES) * (1.0 / RWKV_DH)
        yn = d * lax.rsqrt(var + RWKV_LN_EPS) * lng_ref[...] + lnb_ref[...]
        o_ref[b] = ((yn + bon_ref[b]) * g_ref[b]).astype(o_ref.dtype)

    @pl.when(c == pl.num_programs(2) - 1)
    def _():
        sT_ref[...] = s_scr[...]


def _rwkv_scan(prep, s0, lng, lnb, bsz, t, bb, tc, pg):
    bw = BRANCH_WIDTH
    npair = RWKV_HEADS // 2
    assert tc % SUBLANES == 0 and npair % pg == 0
    wcols = pg * LANES
    seq = [z.reshape(bsz, t, bw) for z in prep]
    blk = pl.BlockSpec((bb, tc, wcols), lambda b, q, c: (b, c, q))
    st = pl.BlockSpec((bb, pg, RWKV_DH, LANES), lambda b, q, c: (b, q, 0, 0))
    vec = pl.BlockSpec((1, wcols), lambda b, q, c: (0, q))
    kern = functools.partial(_rwkv_scan_kernel, bb=bb, tc=tc, pg=pg)
    o, s_t = pl.pallas_call(
        kern,
        grid=(bsz // bb, npair // pg, t // tc),
        in_specs=[blk] * 10 + [st, vec, vec, pl.BlockSpec((LANES, LANES), lambda b, q, c: (0, 0))],
        out_specs=[blk, st],
        out_shape=[jax.ShapeDtypeStruct((bsz, t, bw), BF16),
                   jax.ShapeDtypeStruct((bsz, npair, RWKV_DH, LANES), F32)],
        scratch_shapes=[pltpu.VMEM((bb, pg, RWKV_DH, LANES), F32), pltpu.VMEM((bb, tc, wcols), F32)],
        compiler_params=_cparams(("arbitrary", "arbitrary", "arbitrary")),
        name="rwkv_scan",
    )(*seq, s0, lng, lnb, _block_ones(LANES, RWKV_DH))
    return o.reshape(bsz * t, bw), s_t


def _mamba_kernel(x_ref, conv0_ref, h0_ref, cw_ref, cb_ref, dtb_ref, alog_ref, dvec_ref, nrm_ref,
                  tri_ref, eye_ref, xp_ref, xpt_ref, e256_ref, o_ref, hT_ref, tail_scr, h_scr, *, ln):
    bw = BRANCH_WIDTH
    gn = SSM_GROUPS * SSM_STATE
    npair = SSM_HEADS // 2
    c = pl.program_id(1)

    @pl.when(c == 0)
    def _():
        tail_scr[...] = conv0_ref[...]
        h_scr[...] = h0_ref[...]

    blk = x_ref[...]
    z = blk[:, :bw]
    xbc = blk[:, bw:bw + SSM_CONV_DIM]
    dtr = blk[:, bw + SSM_CONV_DIM:bw + SSM_CONV_DIM + LANES]
    xpad = jnp.concatenate([tail_scr[...], xbc], axis=0)
    conv = cb_ref[...]
    for i in range(SSM_CONV):
        off = SUBLANES - (SSM_CONV - 1) + i
        conv = conv + xpad[off:off + ln] * cw_ref[i:i + 1, :]
    tail_scr[...] = xpad[ln:ln + SUBLANES]
    conv = _silu(conv)
    xs = conv[:, :bw]
    lane = lax.broadcasted_iota(jnp.int32, (ln, LANES), 1)
    dt = jnp.where(lane < SSM_HEADS, _softplus(dtr + dtb_ref[...]), 0.0)
    a_neg = -jnp.exp(alog_ref[...])
    a = dt * a_neg
    ah, am_, al = _split3(a)
    tri = tri_ref[...]
    a_cs = _dot(tri, ah) + _dot(tri, am_) + _dot(tri, al)
    ch, cm, cl = _split3(a_cs)
    eye = eye_ref[...]
    a_cs_t = _dot_nt(eye, ch) + _dot_nt(eye, cm) + _dot_nt(eye, cl)
    a_tot = a_cs[ln - 1:ln, :]
    xp = xp_ref[...]
    dt_x = _dot_x3(dt, xp)
    acs_x = _dot_x3(a_cs, xp)
    atot_x = _dot_x3(jnp.broadcast_to(a_tot, (SUBLANES, LANES)), xp)[0:1]
    th, tm_, tl = _split3(jnp.broadcast_to(a_tot, (SUBLANES, LANES)))
    xpt = xpt_ref[...]
    atot_col = (_dot_nt(xpt, th) + _dot_nt(xpt, tm_) + _dot_nt(xpt, tl))[:, 0:1]
    xc = xs * dt_x
    xd = xc * jnp.exp(atot_x - acs_x)
    row_i = lax.broadcasted_iota(jnp.int32, (ln, ln), 0)
    col_i = lax.broadcasted_iota(jnp.int32, (ln, ln), 1)
    tril = row_i >= col_i
    lane_h = lax.broadcasted_iota(jnp.int32, (ln, LANES), 1) < SSM_HEAD_DIM
    y_parts = []
    for g in range(SSM_GROUPS):
        bg = conv[:, bw + g * SSM_STATE:bw + (g + 1) * SSM_STATE].astype(BF16)
        cg = conv[:, bw + gn + g * SSM_STATE:bw + gn + (g + 1) * SSM_STATE].astype(BF16)
        cbm = _dot_nt(cg, bg)
        for pr in range(2):
            pi = g * 2 + pr
            cs = slice(pi * LANES, (pi + 1) * LANES)
            xcp = xc[:, cs]
            y = None
            for e in range(2):
                hd = 2 * pi + e
                seg = a_cs[:, hd:hd + 1] - a_cs_t[hd:hd + 1, :]
                wm = jnp.where(tril, cbm * jnp.exp(jnp.where(tril, seg, 0.0)), 0.0).astype(BF16)
                xm = jnp.where(lane_h if e == 0 else jnp.logical_not(lane_h), xcp, 0.0).astype(BF16)
                part = _dot(wm, xm)
                y = part if y is None else y + part
            hp = h_scr[pi]
            y = y + _dot_nt(cg, hp.astype(BF16)) * jnp.exp(acs_x[:, cs])
            y_parts.append(y)
            h_new = jnp.exp(atot_col[pi * LANES:(pi + 1) * LANES, :]) * hp + _dot_tn(xd[:, cs].astype(BF16), bg)
            h_scr[pi] = h_new
    y = jnp.concatenate(y_parts, axis=1)
    y = (y + dvec_ref[...] * xs) * _silu(z)
    gw = bw // SSM_GROUPS
    ms = _segsum_bcast(y * y, e256_ref, gw) * (1.0 / gw)
    y = y * lax.rsqrt(ms + 1e-5) * nrm_ref[...]
    o_ref[...] = y.astype(o_ref.dtype)

    @pl.when(c == pl.num_programs(1) - 1)
    def _():
        hT_ref[...] = h_scr[...]


def _mamba(sm, conv0, h0, P, bsz, t):
    bw = BRANCH_WIDTH
    ln = min(SSM_CHUNK, t)
    assert t % ln == 0
    nc = t // ln
    npair = SSM_HEADS // 2
    tri = jnp.asarray(np.tril(np.ones((ln, ln), np.float32)), dtype=BF16)
    eye = jnp.asarray(np.eye(LANES, dtype=np.float32), dtype=BF16)
    xp_np = (np.arange(LANES)[:, None] == (np.arange(bw)[None, :] // SSM_HEAD_DIM)).astype(np.float32)
    xp = jnp.asarray(xp_np, dtype=BF16)
    xpt = jnp.asarray(xp_np.T.copy(), dtype=BF16)
    const = lambda shape: pl.BlockSpec(shape, lambda b, c: tuple(0 for _ in shape))
    kern = functools.partial(_mamba_kernel, ln=ln)
    o, h_t = pl.pallas_call(
        kern,
        grid=(bsz, nc),
        in_specs=[
            pl.BlockSpec((ln, SSM_COLS_PAD), lambda b, c: (b * nc + c, 0)),
            pl.BlockSpec((None, SUBLANES, SSM_CONV_DIM), lambda b, c: (b, 0, 0)),
            pl.BlockSpec((None, npair, LANES, SSM_STATE), lambda b, c: (b, 0, 0, 0)),
            const((SSM_CONV, SSM_CONV_DIM)), const((1, SSM_CONV_DIM)), const((1, LANES)), const((1, LANES)),
            const((1, bw)), const((1, bw)), const((ln, ln)), const((LANES, LANES)),
            const((LANES, bw)), const((bw, LANES)), const((bw // SSM_GROUPS, bw // SSM_GROUPS)),
        ],
        out_specs=[pl.BlockSpec((ln, bw), lambda b, c: (b * nc + c, 0)),
                   pl.BlockSpec((None, npair, LANES, SSM_STATE), lambda b, c: (b, 0, 0, 0))],
        out_shape=[jax.ShapeDtypeStruct((bsz * t, bw), BF16),
                   jax.ShapeDtypeStruct((bsz, npair, LANES, SSM_STATE), F32)],
        scratch_shapes=[pltpu.VMEM((SUBLANES, SSM_CONV_DIM), F32), pltpu.VMEM((npair, LANES, SSM_STATE), F32)],
        compiler_params=_cparams(("arbitrary", "arbitrary")),
        name="mamba",
    )(sm, conv0, h0, P["cw"], P["cb"], P["dtb"], P["alog"], P["dvec"], P["nrm"], tri, eye, xp, xpt,
      _block_ones(bw // SSM_GROUPS, bw // SSM_GROUPS))
    return o, h_t


PEER_NO_RANK = float(PEER_NKEYS)


def _topk_rows(s, k, with_rank=False):
    rows = s.shape[0]
    iota = lax.broadcasted_iota(jnp.int32, s.shape, 0)
    rank = jnp.full(s.shape, PEER_NO_RANK, F32)
    outs = []
    for i in range(k):
        m = jnp.max(s, axis=0, keepdims=True)
        outs.append(m)
        idx = jnp.min(jnp.where(s == m, iota, rows), axis=0, keepdims=True)
        hit = iota == idx
        s = jnp.where(hit, -jnp.inf, s)
        if with_rank:
            rank = jnp.where(hit, float(i), rank)
    top = jnp.concatenate(outs, axis=0)
    return (top, rank) if with_rank else top


def _peer_route_kernel(q_ref, k1_ref, k2_ref, cnt_ref, rank2_ref, e1_ref, e2_ref):
    sc = []
    for half, k_ref in enumerate((k1_ref, k2_ref)):
        kh, kl = _split2(k_ref[...])
        qh, ql = _split2(q_ref[:, half * PEER_HALF:(half + 1) * PEER_HALF])
        sc.append(_dot_nt(kh, qh) + _dot_nt(kh, ql) + _dot_nt(kl, qh))
    v1, rank1 = _topk_rows(sc[0], PEER_TOPK, with_rank=True)
    v2, rank2 = _topk_rows(sc[1], PEER_TOPK, with_rank=True)
    row8 = lax.broadcasted_iota(jnp.int32, (SUBLANES, v2.shape[1]), 0)
    pieces = [v1[0:1, :] + v2]
    for i in range(1, SUBLANES):
        n_i = PEER_TOPK // (i + 1)
        pieces.append(jnp.where(row8 < n_i, v1[i:i + 1, :] + v2[0:SUBLANES, :], -jnp.inf))
    pieces.append(v1[SUBLANES:PEER_TOPK, :] + v2[0:1, :])
    cand = jnp.concatenate(pieces, axis=0)
    top = _topk_rows(cand, PEER_TOPK)
    tau = top[PEER_TOPK - 1:PEER_TOPK, :]
    zsum = jnp.sum(jnp.exp(top - top[0:1, :]), axis=0, keepdims=True)
    cnt = jnp.zeros(rank1.shape, F32)
    for i in range(PEER_TOPK):
        c_i = jnp.sum(((v1[i:i + 1, :] + v2) >= tau).astype(F32), axis=0, keepdims=True)
        cnt = jnp.where(rank1 == float(i), c_i, cnt)
    cnt_ref[...] = cnt
    rank2_ref[...] = rank2.astype(BF16)
    e1_ref[...] = jnp.exp(sc[0] - v1[0:1, :]) / zsum
    e2_ref[...] = jnp.exp(sc[1] - v2[0:1, :]).astype(BF16)


def _peer_route(q, k1, k2, tt):
    n = q.shape[0]
    arr = lambda dt: jax.ShapeDtypeStruct((PEER_HEADS, PEER_NKEYS, n), dt)
    spec = pl.BlockSpec((None, PEER_NKEYS, tt), lambda i, h: (h, 0, i))
    return pl.pallas_call(
        _peer_route_kernel,
        grid=(n // tt, PEER_HEADS),
        in_specs=[pl.BlockSpec((tt, PEER_KEY_DIM), lambda i, h: (i, h)),
                  pl.BlockSpec((PEER_NKEYS, PEER_HALF), lambda i, h: (0, 0)),
                  pl.BlockSpec((PEER_NKEYS, PEER_HALF), lambda i, h: (0, 0))],
        out_specs=[spec, spec, spec, spec],
        out_shape=[arr(F32), arr(BF16), arr(F32), arr(BF16)],
        compiler_params=_cparams(("arbitrary", "arbitrary")),
        name="peer_route",
    )(q, k1, k2)


def _peer_dense_kernel(ht_ref, u_ref, vt_ref, cnt_ref, rank2_ref, e1_ref, e2_ref, o_ref, act_scr, ca_scr,
                       *, ib, nsplit):
    j = pl.program_id(1)

    @pl.when(j == 0)
    def _():
        o_ref[...] = jnp.zeros(o_ref.shape, F32)

    ht = ht_ref[...]
    per = ib // nsplit
    rows = per * PEER_NKEYS
    for part in range(nsplit):
        act_scr[part * rows:(part + 1) * rows, :] = _dot(u_ref[part * rows:(part + 1) * rows, :], ht)
    for part in range(nsplit):
        for ii in range(part * per, (part + 1) * per):
            sl = slice(ii * PEER_NKEYS, (ii + 1) * PEER_NKEYS)
            act = act_scr[sl, :]
            act = 0.5 * act * (1.0 + lax.erf(act * (2.0 ** -0.5)))
            coef = jnp.zeros(act.shape, BF16)
            for h in range(PEER_HEADS):
                keep = rank2_ref[h] < cnt_ref[h, ii:ii + 1, :].astype(BF16)
                coef = coef + jnp.where(keep, e2_ref[h], 0.0) * e1_ref[h, ii:ii + 1, :].astype(BF16)
            ca_scr[sl, :] = coef * act.astype(BF16)
        o_ref[...] += _dot(vt_ref[:, part * rows:(part + 1) * rows], ca_scr[part * rows:(part + 1) * rows, :])


def _peer_dense(ht, u, vt, l, cnt, rank2, e1, e2, tt, ib, nsplit=2):
    d, n = ht.shape
    eb = ib * PEER_NKEYS
    kern = functools.partial(_peer_dense_kernel, ib=ib, nsplit=nsplit)
    full = pl.BlockSpec((PEER_HEADS, PEER_NKEYS, tt), lambda i, j: (0, 0, i))
    part = pl.BlockSpec((PEER_HEADS, ib, tt), lambda i, j: (0, j, i))
    return pl.pallas_call(
        kern,
        grid=(n // tt, PEER_EXPERTS // eb),
        in_specs=[
            pl.BlockSpec((d, tt), lambda i, j: (0, i)),
            pl.BlockSpec((None, eb, d), lambda i, j: (l, j, 0)),
            pl.BlockSpec((None, d, eb), lambda i, j: (l, 0, j)),
            part, full, part, full,
        ],
        out_specs=pl.BlockSpec((d, tt), lambda i, j: (0, i)),
        out_shape=jax.ShapeDtypeStruct((d, n), F32),
        scratch_shapes=[pltpu.VMEM((eb, tt), F32), pltpu.VMEM((eb, tt), BF16)],
        compiler_params=_cparams(("arbitrary", "arbitrary")),
        name="peer_dense",
    )(ht, u, vt, cnt, rank2, e1, e2)


def _peer_res_kernel(x_ref, ot_ref, g_ref, o_ref):
    o_ref[...] = x_ref[...] + g_ref[...] * ot_ref[...].T


def _peer_res(x, out_t, mod, g_blk, tm):
    n, d = x.shape
    nseq, r, _ = mod.shape
    tiles_per_seq = (n // tm) // nseq
    return pl.pallas_call(
        _peer_res_kernel,
        grid=(n // tm,),
        in_specs=[pl.BlockSpec((tm, d), lambda i: (i, 0)),
                  pl.BlockSpec((d, tm), lambda i: (0, i)),
                  pl.BlockSpec((None, r, d), lambda i: (i // tiles_per_seq, 0, g_blk))],
        out_specs=pl.BlockSpec((tm, d), lambda i: (i, 0)),
        out_shape=jax.ShapeDtypeStruct((n, d), F32),
        compiler_params=_cparams(("arbitrary",)),
        name="peer_res",
    )(x, out_t, mod)


def _pad_cols(a, width):
    return jnp.pad(a, [(0, 0)] * (a.ndim - 1) + [(0, width - a.shape[-1])])


def _pad_rows(a, rows):
    return jnp.pad(a, [(0, rows - a.shape[0])] + [(0, 0)] * (a.ndim - 1))


def _big_weights(W):
    w_in = W["w_in"]
    return {
        "w_qkv": w_in[:, :, :RWKV_OFF].astype(BF16),
        "w_rw": _pad_cols(w_in[:, :, RWKV_OFF:SSM_OFF], RWKV_COLS_PAD).astype(BF16),
        "w_sm": _pad_cols(w_in[:, :, SSM_OFF:GATE_OFF], SSM_COLS_PAD).astype(BF16),
        "w_gate": w_in[:, :, GATE_OFF:].astype(BF16),
        "wb": W["w_branch"].astype(BF16),
        "wo": W["w_o"].astype(BF16),
        "wq": W["peer_wq"].astype(BF16),
        "u": W["peer_u"].astype(BF16),
        "vt": jnp.swapaxes(W["peer_v"].astype(BF16), 1, 2),
    }


def _layer_weights(l, W, big):
    bw = BRANCH_WIDTH
    out = dict(big)
    o1 = 3 * bw
    o2 = o1 + RWKV_DECAY_LORA
    o3 = o2 + RWKV_AAA_LORA
    rwp = {
        "mu": _pad_cols(W["rwkv_mu"][l][None, :], RWKV_COLS_PAD),
        "w0": W["rwkv_w0"][l][None, :],
        "w2": _pad_rows(W["rwkv_w2"][l], LANES).astype(BF16),
        "a0": W["rwkv_a0"][l][None, :],
        "a2": jnp.concatenate([jnp.zeros((RWKV_DECAY_LORA, bw), F32), W["rwkv_a2"][l]], axis=0).astype(BF16),
        "g2": _pad_rows(W["rwkv_g2"][l], 2 * LANES).astype(BF16),
        "kk": W["rwkv_kk"][l][None, :],
        "ka": W["rwkv_ka"][l][None, :],
        "rk": W["rwkv_rk"][l].reshape(1, bw),
        "lng": W["rwkv_lnx_g"][l][None, :],
        "lnb": W["rwkv_lnx_b"][l][None, :],
    }
    smp = {
        "cw": W["ssm_conv_w"][l],
        "cb": W["ssm_conv_b"][l][None, :],
        "dtb": _pad_cols(W["ssm_dt_bias"][l][None, :], LANES),
        "alog": _pad_cols(W["ssm_a_log"][l][None, :], LANES),
        "dvec": jnp.repeat(W["ssm_d"][l], SSM_HEAD_DIM)[None, :],
        "nrm": W["ssm_norm"][l][None, :],
    }
    lv = W["att_lambda"][l].astype(F32)
    lam_init = 0.8 - 0.6 * math.exp(-0.3 * l)
    lam = jnp.exp(jnp.sum(lv[0] * lv[1])) - jnp.exp(jnp.sum(lv[2] * lv[3])) + lam_init
    out.update(rwp=rwp, smp=smp, lam=lam, lam_init=lam_init)
    return out


def _layer(l, x, mod, lw, W, bsz, t, state, cfg, paged):
    n = bsz * t
    tm, tn = cfg["tm"], cfg["tn"]
    shift0, wkv0, conv0, ssm0 = state
    mm = lambda w, tn_, dt=F32: _normmm(x, W["norm1"][l], mod, 1, 0, w, l, tm, tn_, dt)
    qkv = mm(lw["w_qkv"], 512)
    rw = mm(lw["w_rw"], 384)
    sm = mm(lw["w_sm"], 640)
    gates = mm(lw["w_gate"], 512, BF16)
    if paged is None:
        o_att = _attn_prompt(qkv, W["rel_bias"], lw["lam"], W["att_subln"][l], lw["lam_init"], bsz, t, cfg["tq"])
    else:
        cache_k, cache_v, page_table = paged
        o_att = _attn_sample(qkv, cache_k, cache_v, page_table, l, W["rel_bias"], lw["lam"], W["att_subln"][l],
                             lw["lam_init"], bsz, t)
    prep = _rwkv_prep(rw, shift0, lw["rwp"], bsz, t, cfg["rw_tm"])
    o_rwkv, wkv1 = _rwkv_scan(prep, wkv0, lw["rwp"]["lng"], lw["rwp"]["lnb"], bsz, t, cfg["rw_bb"], cfg["rw_tc"],
                               cfg["rw_pg"])
    o_ssm, ssm1 = _mamba(sm, conv0, ssm0, lw["smp"], bsz, t)
    merged = _merge(o_att, o_rwkv, o_ssm, gates, lw["wb"], l, cfg["tm2"], tn)
    x = _proj_res(merged, lw["wo"], l, x, mod, 2, cfg["tm2"], tn)
    npad = -(-n // LANES) * LANES
    xq, modq, tmq = x, mod, tm
    if npad != n:
        xq = _pad_rows(x, npad)
        modq = jnp.pad(mod, ((0, 0), (0, npad - n), (0, 0)))
        tmq = npad
    q = _normmm(xq, W["norm2"][l], modq, 4, 3, lw["wq"], l, tmq, tn)
    ht = _normT(xq, W["norm2"][l], modq, 4, 3, cfg["tmT"])
    cnt, rank2, e1, e2 = _peer_route(q, W["peer_k1"][l], W["peer_k2"][l], cfg["tt_route"])
    out_t = _peer_dense(ht, lw["u"], lw["vt"], l, cnt, rank2, e1, e2, cfg["tt_dense"], cfg["ib"])
    x = _peer_res(xq, out_t, modq, 5, cfg["tmT"])[:n]
    qkv3 = qkv.reshape(bsz, t, 3 * ATT_QK)
    k_new = qkv3[:, :, ATT_QK:2 * ATT_QK].reshape(bsz, t, ATT_HEADS, 2 * ATT_DH)
    v_new = qkv3[:, :, 2 * ATT_QK:].reshape(bsz, t, ATT_HEADS, ATT_VD)
    shift1 = rw.reshape(bsz, t, RWKV_COLS_PAD)[:, -1, :RWKV_COLS]
    assert t >= SSM_CONV - 1
    conv1 = sm.reshape(bsz, t, SSM_COLS_PAD)[:, -(SSM_CONV - 1):, BRANCH_WIDTH:BRANCH_WIDTH + SSM_CONV_DIM]
    return x, (k_new, v_new, shift1, wkv1, conv1, ssm1)


def _wkv_to_kernel(s):
    b = s.shape[0]
    s = s.reshape(b, RWKV_HEADS // 2, 2, RWKV_DH, RWKV_DH)
    return jnp.transpose(s, (0, 1, 3, 2, 4)).reshape(b, RWKV_HEADS // 2, RWKV_DH, 2 * RWKV_DH)


def _wkv_from_kernel(s):
    b = s.shape[0]
    s = s.reshape(b, RWKV_HEADS // 2, RWKV_DH, 2, RWKV_DH)
    return jnp.transpose(s, (0, 1, 3, 2, 4)).reshape(b, RWKV_HEADS, RWKV_DH, RWKV_DH)


def _run_group(x, mod_of_layer, W, lws, bsz, t, states, cfg, paged_of_layer):
    xs = x.reshape(bsz * t, D_MODEL)
    outs = []
    for l in range(DEPTH):
        xs, new = _layer(l, xs, mod_of_layer(l), lws[l], W, bsz, t, states[l], cfg, paged_of_layer(l))
        outs.append(new)
    y = _final_norm(xs, W["norm_f"], cfg["tm_f"]).reshape(bsz, t, D_MODEL)
    k = jnp.stack([o[0] for o in outs], axis=1)
    v = jnp.stack([o[1] for o in outs], axis=1)
    shift = jnp.stack([o[2] for o in outs], axis=0)
    wkv = jnp.stack([_wkv_from_kernel(o[3]) for o in outs], axis=0)
    conv = jnp.stack([o[4] for o in outs], axis=0)
    ssm = jnp.stack([o[5].reshape(bsz, SSM_HEADS, SSM_HEAD_DIM, SSM_STATE) for o in outs], axis=0)
    return y, k, v, shift, wkv, conv, ssm


PROMPT_CFG = dict(tm=1024, tn=512, tm2=512, tq=512, rw_tm=256, rw_bb=2, rw_tc=128, rw_pg=4, tmT=256,
                  tt_route=256, tt_dense=512, ib=8, tm_f=512)


def _sample_cfg(n, t):
    npad = -(-n // LANES) * LANES
    return dict(tm=n, tn=512, tm2=n, tq=None, rw_tm=t, rw_bb=2, rw_tc=t, rw_pg=4, tmT=npad,
                tt_route=npad, tt_dense=npad, ib=8, tm_f=n)


def kernel(x_prompt, x_sample, cache_k, cache_v, state_rwkv_shift, state_rwkv, state_conv, state_ssm, page_table, c_prompt, c_sample, w_ada, b_ada, norm1, norm2, norm_f, w_in, rel_bias, att_lambda, att_subln, rwkv_mu, rwkv_w0, rwkv_w2, rwkv_a0, rwkv_a2, rwkv_g2, rwkv_kk, rwkv_ka, rwkv_rk, rwkv_lnx_g, rwkv_lnx_b, ssm_conv_w, ssm_conv_b, ssm_dt_bias, ssm_a_log, ssm_d, ssm_norm, w_branch, w_o, peer_wq, peer_k1, peer_k2, peer_u, peer_v):
    W = dict(norm1=norm1, norm2=norm2, norm_f=norm_f, w_in=w_in, rel_bias=rel_bias, att_lambda=att_lambda,
             att_subln=att_subln, rwkv_mu=rwkv_mu, rwkv_w0=rwkv_w0, rwkv_w2=rwkv_w2, rwkv_a0=rwkv_a0,
             rwkv_a2=rwkv_a2, rwkv_g2=rwkv_g2, rwkv_kk=rwkv_kk, rwkv_ka=rwkv_ka, rwkv_rk=rwkv_rk,
             rwkv_lnx_g=rwkv_lnx_g, rwkv_lnx_b=rwkv_lnx_b, ssm_conv_w=ssm_conv_w, ssm_conv_b=ssm_conv_b,
             ssm_dt_bias=ssm_dt_bias, ssm_a_log=ssm_a_log, ssm_d=ssm_d, ssm_norm=ssm_norm, w_branch=w_branch,
             w_o=w_o, peer_wq=peer_wq, peer_k1=peer_k1, peer_k2=peer_k2, peer_u=peer_u, peer_v=peer_v)
    bp, tp, _ = x_prompt.shape
    bs, ts, _ = x_sample.shape
    nseq = bp + bs
    rows = -(-nseq // SUBLANES) * SUBLANES
    c_all = _pad_rows(jnp.concatenate([c_prompt, c_sample], axis=0), rows)
    mod = _ada(c_all, w_ada, b_ada)
    big = _big_weights(W)
    lws = [_layer_weights(l, W, big) for l in range(DEPTH)]
    npair = RWKV_HEADS // 2
    zero_state = (jnp.zeros((bp, RWKV_COLS_PAD), F32),
                  jnp.zeros((bp, npair, RWKV_DH, LANES), F32),
                  jnp.zeros((bp, SUBLANES, SSM_CONV_DIM), F32),
                  jnp.zeros((bp, SSM_HEADS // 2, LANES, SSM_STATE), F32))
    outs_p = _run_group(x_prompt, lambda l: mod[l, :bp].reshape(bp, 1, 6 * D_MODEL), W, lws, bp, tp,
                        [zero_state] * DEPTH, PROMPT_CFG, lambda l: None)
    states_s = []
    for l in range(DEPTH):
        states_s.append((
            _pad_cols(state_rwkv_shift[l], RWKV_COLS_PAD),
            _wkv_to_kernel(state_rwkv[l]),
            jnp.pad(state_conv[l], ((0, 0), (SUBLANES - (SSM_CONV - 1), 0), (0, 0))),
            state_ssm[l].reshape(bs, SSM_HEADS // 2, LANES, SSM_STATE),
        ))
    ns = bs * ts
    mod_s = lambda l: jnp.repeat(mod[l, bp:bp + bs], ts, axis=0).reshape(1, ns, 6 * D_MODEL)
    outs_s = _run_group(x_sample, mod_s, W, lws, bs, ts, states_s, _sample_cfg(ns, ts),
                        lambda l: (cache_k, cache_v, page_table))
    return (outs_p[0], outs_s[0], *outs_p[1:], *outs_s[1:])
```

```python
import functools
import math

import jax
import jax.numpy as jnp
import numpy as np
from jax import lax
from jax.experimental import pallas as pl
from jax.experimental.pallas import tpu as pltpu

F32 = jnp.float32
BF16 = jnp.bfloat16

LANES = 128
SUBLANES = 8
VMEM_LIMIT_BYTES = 56 * 1024 * 1024

D_MODEL = 2048
DEPTH = 4
PAGE_SIZE = 128
BRANCH_WIDTH = D_MODEL // 2
N_BRANCHES = 3
ATT_DH = 64
ATT_VD = 2 * ATT_DH
ATT_HEADS = BRANCH_WIDTH // ATT_VD
ATT_QK = ATT_HEADS * 2 * ATT_DH
REL_BUCKETS = 32
REL_MAX_DIST = 128
RWKV_DH = 64
RWKV_HEADS = BRANCH_WIDTH // RWKV_DH
RWKV_DECAY_LORA = 64
RWKV_AAA_LORA = 64
RWKV_GATE_LORA = 160
RWKV_LN_EPS = 64e-5
RWKV_COLS = 3 * BRANCH_WIDTH + RWKV_DECAY_LORA + RWKV_AAA_LORA + RWKV_GATE_LORA
RWKV_COLS_PAD = 3456
SSM_HEAD_DIM = 64
SSM_HEADS = BRANCH_WIDTH // SSM_HEAD_DIM
SSM_GROUPS = 4
SSM_STATE = 128
SSM_CONV = 4
SSM_CONV_DIM = BRANCH_WIDTH + 2 * SSM_GROUPS * SSM_STATE
SSM_CHUNK = 128
SSM_COLS = BRANCH_WIDTH + SSM_CONV_DIM + SSM_HEADS
SSM_COLS_PAD = 3200
PEER_HEADS = 8
PEER_NKEYS = 128
PEER_EXPERTS = PEER_NKEYS * PEER_NKEYS
PEER_KEY_DIM = 256
PEER_HALF = PEER_KEY_DIM // 2
PEER_TOPK = 16
ATT_Q_OFF = 0
ATT_K_OFF = ATT_QK
ATT_V_OFF = 2 * ATT_QK
RWKV_OFF = ATT_V_OFF + ATT_HEADS * ATT_VD
SSM_OFF = RWKV_OFF + RWKV_COLS
GATE_OFF = SSM_OFF + SSM_COLS

NEG_BIG = -1e30


def _cparams(semantics):
    return pltpu.CompilerParams(dimension_semantics=semantics, vmem_limit_bytes=VMEM_LIMIT_BYTES)


def _dot(a, b):
    return jnp.dot(a, b, preferred_element_type=F32)


def _dot_nt(a, b):
    return lax.dot_general(a, b, (((1,), (1,)), ((), ())), preferred_element_type=F32)


def _dot_tn(a, b):
    return lax.dot_general(a, b, (((0,), (0,)), ((), ())), preferred_element_type=F32)


def _split2(x):
    hi = x.astype(BF16)
    lo = (x - hi.astype(F32)).astype(BF16)
    return hi, lo


def _split3(x):
    hi = x.astype(BF16)
    r = x - hi.astype(F32)
    mid = r.astype(BF16)
    lo = (r - mid.astype(F32)).astype(BF16)
    return hi, mid, lo


def _dot_x3(x, w_bf16):
    hi, mid, lo = _split3(x)
    return _dot(hi, w_bf16) + _dot(mid, w_bf16) + _dot(lo, w_bf16)


def _dot_x2(x, w_bf16):
    hi, lo = _split2(x)
    return _dot(hi, w_bf16) + _dot(lo, w_bf16)


def _block_ones(n, width):
    r = np.arange(n) // width
    return jnp.asarray((r[:, None] == r[None, :]).astype(np.float32), dtype=BF16)


def _segsum_bcast(x, ones_ref, width_block):
    cols = x.shape[1]
    outs = []
    e = ones_ref[...]
    for c in range(cols // width_block):
        outs.append(_dot_x2(x[:, c * width_block:(c + 1) * width_block], e))
    return outs[0] if len(outs) == 1 else jnp.concatenate(outs, axis=1)


def _silu(x):
    return x * jax.nn.sigmoid(x)


def _softplus(x):
    return jnp.maximum(x, 0.0) + jnp.log1p(jnp.exp(-jnp.abs(x)))


def _ada_kernel(c_ref, w_ref, b_ref, o_ref):
    c = c_ref[...]
    a = _silu(c).astype(BF16)
    o_ref[...] = _dot(a, w_ref[...].astype(BF16)) + b_ref[...]


def _ada(c_all, w_ada, b_ada):
    rows = c_all.shape[0]
    tn = 1024
    ncol = w_ada.shape[2]
    return pl.pallas_call(
        _ada_kernel,
        grid=(DEPTH, ncol // tn),
        in_specs=[
            pl.BlockSpec((rows, D_MODEL), lambda l, j: (0, 0)),
            pl.BlockSpec((None, D_MODEL, tn), lambda l, j: (l, 0, j)),
            pl.BlockSpec((None, 1, tn), lambda l, j: (l, 0, j)),
        ],
        out_specs=pl.BlockSpec((None, rows, tn), lambda l, j: (l, 0, j)),
        out_shape=jax.ShapeDtypeStruct((DEPTH, rows, ncol), F32),
        compiler_params=_cparams(("arbitrary", "arbitrary")),
        name="ada",
    )(c_all, w_ada, b_ada.reshape(DEPTH, 1, ncol))


def _norm_rows(x, g, eps):
    return x * lax.rsqrt(jnp.mean(x * x, axis=-1, keepdims=True) + eps) * g


def _norm_kernel(x_ref, g_ref, sc_ref, sh_ref, h_ref, *maybe_ht_ref):
    h = _norm_rows(x_ref[...], g_ref[...], 1e-6) * (1.0 + sc_ref[...]) + sh_ref[...]
    h_ref[...] = h.astype(BF16)
    for ht_ref in maybe_ht_ref:
        ht_ref[...] = h.T.astype(BF16)


def _norm(x, g, mod, sc_blk, sh_blk, tm, with_transpose=False):
    n, d = x.shape
    nseq, r, _ = mod.shape
    tiles_per_seq = (n // tm) // nseq
    out_specs = [pl.BlockSpec((tm, d), lambda i: (i, 0))]
    out_shape = [jax.ShapeDtypeStruct((n, d), BF16)]
    if with_transpose:
        out_specs.append(pl.BlockSpec((d, tm), lambda i: (0, i)))
        out_shape.append(jax.ShapeDtypeStruct((d, n), BF16))
    return pl.pallas_call(
        _norm_kernel,
        grid=(n // tm,),
        in_specs=[
            pl.BlockSpec((tm, d), lambda i: (i, 0)),
            pl.BlockSpec((1, d), lambda i: (0, 0)),
            pl.BlockSpec((None, r, d), lambda i: (i // tiles_per_seq, 0, sc_blk)),
            pl.BlockSpec((None, r, d), lambda i: (i // tiles_per_seq, 0, sh_blk)),
        ],
        out_specs=out_specs,
        out_shape=out_shape,
        compiler_params=_cparams(("arbitrary",)),
        name="norm",
    )(x, g.reshape(1, d), mod, mod)


def _mm_kernel(a_ref, w_ref, o_ref):
    o_ref[...] = _dot(a_ref[...], w_ref[...]).astype(o_ref.dtype)


def _mm(a, w, l, tm, tn, out_dtype=F32):
    n, d = a.shape
    ncol = w.shape[2]
    return pl.pallas_call(
        _mm_kernel,
        grid=(n // tm, ncol // tn),
        in_specs=[
            pl.BlockSpec((tm, d), lambda i, j: (i, 0)),
            pl.BlockSpec((None, d, tn), lambda i, j: (l, 0, j)),
        ],
        out_specs=pl.BlockSpec((tm, tn), lambda i, j: (i, j)),
        out_shape=jax.ShapeDtypeStruct((n, ncol), out_dtype),
        compiler_params=_cparams(("arbitrary", "arbitrary")),
        name="mm",
    )(a, w)


def _final_norm_kernel(x_ref, g_ref, o_ref):
    o_ref[...] = _norm_rows(x_ref[...], g_ref[...], 1e-6)


def _final_norm(x, g, tm):
    n, d = x.shape
    return pl.pallas_call(
        _final_norm_kernel,
        grid=(n // tm,),
        in_specs=[pl.BlockSpec((tm, d), lambda i: (i, 0)), pl.BlockSpec((1, d), lambda i: (0, 0))],
        out_specs=pl.BlockSpec((tm, d), lambda i: (i, 0)),
        out_shape=jax.ShapeDtypeStruct((n, d), F32),
        compiler_params=_cparams(("arbitrary",)),
        name="final_norm",
    )(x, g.reshape(1, d))


def _merge_kernel(o0_ref, o1_ref, o2_ref, g0_ref, g1_ref, g2_ref, w_ref, out_ref):
    acc = jax.nn.sigmoid(g0_ref[...].astype(F32)) * _dot(o0_ref[...], w_ref[0])
    acc += jax.nn.sigmoid(g1_ref[...].astype(F32)) * _dot(o1_ref[...], w_ref[1])
    acc += jax.nn.sigmoid(g2_ref[...].astype(F32)) * _dot(o2_ref[...], w_ref[2])
    out_ref[...] = acc.astype(out_ref.dtype)


def _merge(o_att, o_rwkv, o_ssm, gates, wb, l, tm, tn):
    n = o_att.shape[0]
    gblk = D_MODEL // tn
    o_spec = pl.BlockSpec((tm, BRANCH_WIDTH), lambda i, j: (i, 0))
    g_specs = [pl.BlockSpec((tm, tn), functools.partial(lambda i, j, b: (i, b * gblk + j), b=b)) for b in range(3)]
    return pl.pallas_call(
        _merge_kernel,
        grid=(n // tm, D_MODEL // tn),
        in_specs=[o_spec, o_spec, o_spec, *g_specs,
                  pl.BlockSpec((None, N_BRANCHES, BRANCH_WIDTH, tn), lambda i, j: (l, 0, 0, j))],
        out_specs=pl.BlockSpec((tm, tn), lambda i, j: (i, j)),
        out_shape=jax.ShapeDtypeStruct((n, D_MODEL), BF16),
        compiler_params=_cparams(("arbitrary", "arbitrary")),
        name="merge",
    )(o_att, o_rwkv, o_ssm, gates, gates, gates, wb)


def _proj_res_kernel(a_ref, w_ref, x_ref, g_ref, o_ref):
    o_ref[...] = x_ref[...] + g_ref[...] * _dot(a_ref[...], w_ref[...])


def _proj_res(a, w, l, x, mod, g_blk, tm, tn):
    n, k = a.shape
    nseq, r, _ = mod.shape
    tiles_per_seq = (n // tm) // nseq
    nblk = D_MODEL // tn
    return pl.pallas_call(
        _proj_res_kernel,
        grid=(n // tm, D_MODEL // tn),
        in_specs=[
            pl.BlockSpec((tm, k), lambda i, j: (i, 0)),
            pl.BlockSpec((None, k, tn), lambda i, j: (l, 0, j)),
            pl.BlockSpec((tm, tn), lambda i, j: (i, j)),
            pl.BlockSpec((None, r, tn), lambda i, j: (i // tiles_per_seq, 0, g_blk * nblk + j)),
        ],
        out_specs=pl.BlockSpec((tm, tn), lambda i, j: (i, j)),
        out_shape=jax.ShapeDtypeStruct((n, D_MODEL), F32),
        compiler_params=_cparams(("arbitrary", "arbitrary")),
        name="proj_res",
    )(a, w, x, mod)


def _rel_bucket(dist):
    n = jnp.maximum(dist, 0)
    max_exact = REL_BUCKETS // 2
    nf = jnp.maximum(n, 1).astype(F32)
    large = max_exact + (jnp.log(nf / max_exact) / math.log(REL_MAX_DIST / max_exact)
                         * (REL_BUCKETS - max_exact)).astype(jnp.int32)
    large = jnp.minimum(large, REL_BUCKETS - 1)
    return jnp.where(n < max_exact, n, large)


def _bias_lookup(rel_bias, dist):
    onehot = (_rel_bucket(dist)[..., None] == jnp.arange(REL_BUCKETS, dtype=jnp.int32)).astype(F32)
    return jnp.einsum("...b,bh->...h", onehot, rel_bias.astype(F32), precision=lax.Precision.HIGHEST)


def _bias_of_dist(rel_bias, dist):
    b = jnp.moveaxis(_bias_lookup(rel_bias, dist), -1, 0)
    return jnp.where(dist[None] >= 0, b, NEG_BIG)


def _softmax_step(s, v_bf16, m_scr, l_scr, acc_scr):
    width = min(LANES, s.shape[1])
    cols = [s[:, c:c + width] for c in range(0, s.shape[1], width)]
    m_prev = m_scr[...]
    m_new = jnp.maximum(m_prev, jnp.max(functools.reduce(jnp.maximum, cols), axis=-1, keepdims=True))
    alpha = jnp.exp(m_prev - m_new)
    ps = [jnp.exp(c - m_new[:, :width]) for c in cols]
    l_scr[...] = alpha * l_scr[...] + jnp.sum(functools.reduce(jnp.add, ps), axis=-1, keepdims=True)
    p = ps[0] if len(ps) == 1 else jnp.concatenate(ps, axis=1)
    acc_scr[...] = alpha * acc_scr[...] + _dot(p.astype(BF16), v_bf16)
    m_scr[...] = m_new


def _two_map_queries(q, scale):
    lane = lax.broadcasted_iota(jnp.int32, q.shape, 1)
    qs = q * scale
    q0 = jnp.where(lane < ATT_DH, qs, 0.0)
    q1 = jnp.where(lane >= ATT_DH, qs, 0.0)
    return jnp.concatenate([q0, q1], axis=0).astype(BF16)


def _diff_finalize(acc, l, lam, g, rows, post_scale):
    o0 = acc[:rows] / l[:rows]
    o1 = acc[rows:] / l[rows:]
    o = o0 - lam * o1
    o = o * lax.rsqrt(jnp.mean(o * o, axis=-1, keepdims=True) + 1e-5) * g
    return o * post_scale


def _attn_prompt_kernel(qi_ref, ki_ref, lam_ref, q_ref, k_ref, v_ref, bias_ref, g_ref, o_ref,
                        m_scr, l_scr, acc_scr, *, tq, hps, rb, post_scale):
    p = pl.program_id(2)
    qi = qi_ref[p]
    ki = ki_ref[p]

    @pl.when(ki == 0)
    def _():
        m_scr[...] = jnp.full(m_scr.shape, NEG_BIG, F32)
        l_scr[...] = jnp.zeros(l_scr.shape, F32)
        acc_scr[...] = jnp.zeros(acc_scr.shape, F32)

    for hh in range(hps):
        cols = slice(hh * LANES, (hh + 1) * LANES)
        kb = k_ref[:, cols].astype(BF16)
        vb = v_ref[:, cols].astype(BF16)
        for r0 in range(0, tq, rb):
            qs = q_ref[r0:r0 + rb, cols] * (ATT_DH ** -0.5)
            lane = lax.broadcasted_iota(jnp.int32, qs.shape, 1)
            b = bias_ref[hh, r0:r0 + rb, :]
            for mp in range(2):
                qm = jnp.where((lane < ATT_DH) if mp == 0 else (lane >= ATT_DH), qs, 0.0).astype(BF16)
                rows = pl.ds(mp * tq + r0, rb)
                _softmax_step(_dot_nt(qm, kb) + b, vb, m_scr.at[hh, rows], l_scr.at[hh, rows], acc_scr.at[hh, rows])

    @pl.when(ki == qi)
    def _():
        for hh in range(hps):
            o = _diff_finalize(acc_scr[hh], l_scr[hh], lam_ref[0, 0], g_ref[...], tq, post_scale)
            o_ref[:, hh * LANES:(hh + 1) * LANES] = o.astype(o_ref.dtype)


def _attn_prompt(qkv, rel_bias, lam, subln, lam_init, bsz, t, tq, hps=2, rb=None):
    rb = tq if rb is None else rb
    nq = t // tq
    pairs = [(a, b) for a in range(nq) for b in range(a + 1)]
    qi_tab = jnp.asarray([a for a, _ in pairs], jnp.int32)
    ki_tab = jnp.asarray([b for _, b in pairs], jnp.int32)
    assert tq % REL_MAX_DIST == 0
    nblk = tq // REL_MAX_DIST
    ii = jnp.arange(REL_MAX_DIST, dtype=jnp.int32)
    d0 = ii[:, None] - ii[None, :]
    blocks = [_bias_of_dist(rel_bias, d0 + k * REL_MAX_DIST) for k in range(3)]
    masked = jnp.full_like(blocks[0], NEG_BIG)

    def tile(tile_dist):
        rows = []
        for bi in range(nblk):
            row = []
            for bj in range(nblk):
                dblk = tile_dist * nblk + bi - bj
                row.append(masked if dblk < 0 else blocks[min(dblk, 2)])
            rows.append(jnp.concatenate(row, axis=2))
        return jnp.concatenate(rows, axis=1)

    tiles = jnp.stack([tile(0), tile(1), tile(2)], axis=1)
    qkv3 = qkv.reshape(bsz, t, 3 * ATT_QK)
    kern = functools.partial(_attn_prompt_kernel, tq=tq, hps=hps, rb=rb, post_scale=1.0 - lam_init)
    assert ATT_HEADS % hps == 0
    hgroups = ATT_HEADS // hps
    wide = hps * LANES
    grid_spec = pltpu.PrefetchScalarGridSpec(
        num_scalar_prefetch=2,
        grid=(bsz, hgroups, len(pairs)),
        in_specs=[
            pl.BlockSpec(memory_space=pltpu.SMEM),
            pl.BlockSpec((None, tq, wide), lambda b, h, p, qt, kt: (b, qt[p], h)),
            pl.BlockSpec((None, tq, wide), lambda b, h, p, qt, kt: (b, kt[p], hgroups + h)),
            pl.BlockSpec((None, tq, wide), lambda b, h, p, qt, kt: (b, kt[p], 2 * hgroups + h)),
            pl.BlockSpec((hps, None, tq, tq), lambda b, h, p, qt, kt: (h, jnp.minimum(qt[p] - kt[p], 2), 0, 0)),
            pl.BlockSpec((1, LANES), lambda b, h, p, qt, kt: (0, 0)),
        ],
        out_specs=pl.BlockSpec((None, tq, wide), lambda b, h, p, qt, kt: (b, qt[p], h)),
        scratch_shapes=[pltpu.VMEM((hps, 2 * tq, LANES), F32), pltpu.VMEM((hps, 2 * tq, LANES), F32),
                        pltpu.VMEM((hps, 2 * tq, LANES), F32)],
    )
    out = pl.pallas_call(
        kern,
        grid_spec=grid_spec,
        out_shape=jax.ShapeDtypeStruct((bsz, t, BRANCH_WIDTH), BF16),
        compiler_params=_cparams(("arbitrary", "arbitrary", "arbitrary")),
        name="attn_prompt",
    )(qi_tab, ki_tab, lam.reshape(1, 1), qkv3, qkv3, qkv3, tiles, subln.reshape(1, ATT_VD))
    return out.reshape(bsz * t, BRANCH_WIDTH)


def _attn_sample_kernel(pt_ref, lam_ref, q_ref, kn_ref, vn_ref, bias_ref, biasn_ref, g_ref, *rest,
                        pps, rows, post_scale):
    kp_refs = rest[:pps]
    vp_refs = rest[pps:2 * pps]
    o_ref, m_scr, l_scr, acc_scr = rest[2 * pps:]
    s_idx = pl.program_id(1)
    last = s_idx == pl.num_programs(1) - 1

    @pl.when(s_idx == 0)
    def _():
        m_scr[...] = jnp.full(m_scr.shape, NEG_BIG, F32)
        l_scr[...] = jnp.zeros(l_scr.shape, F32)
        acc_scr[...] = jnp.zeros(acc_scr.shape, F32)

    qq = _two_map_queries(q_ref[...], ATT_DH ** -0.5)
    scores = []
    values = []
    for r in range(pps):
        kb = kp_refs[r][...].reshape(PAGE_SIZE * ATT_HEADS, LANES).astype(BF16)
        values.append(vp_refs[r][...].reshape(PAGE_SIZE * ATT_HEADS, LANES).astype(BF16))
        s = _dot_nt(qq, kb)
        if r == pps - 1:
            b = jnp.where(last, bias_ref[1], bias_ref[0])
        else:
            b = bias_ref[0]
        scores.append(s + jnp.concatenate([b, b], axis=0))
    _softmax_step(jnp.concatenate(scores, axis=1), jnp.concatenate(values, axis=0), m_scr, l_scr, acc_scr)

    @pl.when(last)
    def _():
        s = _dot_nt(qq, kn_ref[...].astype(BF16))
        b = biasn_ref[...]
        s = s + jnp.concatenate([b, b], axis=0)
        _softmax_step(s, vn_ref[...].astype(BF16), m_scr, l_scr, acc_scr)
        o = _diff_finalize(acc_scr[...], l_scr[...], lam_ref[0, 0], g_ref[...], rows, post_scale)
        o_ref[...] = o.astype(o_ref.dtype)


def _attn_sample(qkv, cache_k, cache_v, page_table, layer, rel_bias, lam, subln, lam_init, bsz, t):
    n_pages = page_table.shape[1]
    past = n_pages * PAGE_SIZE
    pps = 8
    assert n_pages % pps == 0
    rows = t * ATT_HEADS
    q = qkv[:, :ATT_QK].reshape(bsz, rows, LANES)
    kn = qkv[:, ATT_QK:2 * ATT_QK].reshape(bsz, rows, LANES)
    vn = qkv[:, 2 * ATT_QK:].reshape(bsz, rows, LANES)
    tok = jnp.repeat(jnp.arange(t, dtype=jnp.int32), ATT_HEADS)
    hq = jnp.tile(jnp.arange(ATT_HEADS, dtype=jnp.int32), t)
    kk = jnp.repeat(jnp.arange(PAGE_SIZE, dtype=jnp.int32), ATT_HEADS)
    hk = jnp.tile(jnp.arange(ATT_HEADS, dtype=jnp.int32), PAGE_SIZE)

    head_onehot = (hq[:, None] == jnp.arange(ATT_HEADS, dtype=jnp.int32)).astype(F32)

    def tile(dist, hk_):
        b = jnp.sum(_bias_lookup(rel_bias, dist) * head_onehot[:, None, :], axis=-1)
        return jnp.where((dist >= 0) & (hq[:, None] == hk_[None, :]), b, NEG_BIG)

    assert PAGE_SIZE >= REL_MAX_DIST
    far = tile(jnp.broadcast_to(jnp.int32(2 * PAGE_SIZE), (rows, PAGE_SIZE * ATT_HEADS)) + tok[:, None] - kk[None, :], hk)
    near = tile(PAGE_SIZE + tok[:, None] - kk[None, :], hk)
    bias_pages = jnp.stack([far, near])
    bias_new = tile(tok[:, None] - tok[None, :], hq)
    kern = functools.partial(_attn_sample_kernel, pps=pps, rows=rows, post_scale=1.0 - lam_init)
    page_block = (None, None, PAGE_SIZE, ATT_HEADS, LANES)
    page_specs = [pl.BlockSpec(page_block, functools.partial(lambda b, s, pt, r: (pt[b, s * pps + r], layer, 0, 0, 0), r=r))
                  for r in range(pps)]
    grid_spec = pltpu.PrefetchScalarGridSpec(
        num_scalar_prefetch=1,
        grid=(bsz, n_pages // pps),
        in_specs=[
            pl.BlockSpec(memory_space=pltpu.SMEM),
            pl.BlockSpec((None, rows, LANES), lambda b, s, pt: (b, 0, 0)),
            pl.BlockSpec((None, rows, LANES), lambda b, s, pt: (b, 0, 0)),
            pl.BlockSpec((None, rows, LANES), lambda b, s, pt: (b, 0, 0)),
            pl.BlockSpec((2, rows, PAGE_SIZE * ATT_HEADS), lambda b, s, pt: (0, 0, 0)),
            pl.BlockSpec((rows, rows), lambda b, s, pt: (0, 0)),
            pl.BlockSpec((1, LANES), lambda b, s, pt: (0, 0)),
            *page_specs, *page_specs,
        ],
        out_specs=pl.BlockSpec((None, rows, LANES), lambda b, s, pt: (b, 0, 0)),
        scratch_shapes=[pltpu.VMEM((2 * rows, LANES), F32), pltpu.VMEM((2 * rows, LANES), F32),
                        pltpu.VMEM((2 * rows, LANES), F32)],
    )
    out = pl.pallas_call(
        kern,
        grid_spec=grid_spec,
        out_shape=jax.ShapeDtypeStruct((bsz, rows, LANES), BF16),
        compiler_params=_cparams(("arbitrary", "arbitrary")),
        name="attn_sample",
    )(page_table, lam.reshape(1, 1), q, kn, vn, bias_pages, bias_new, subln.reshape(1, ATT_VD),
      *([cache_k] * pps), *([cache_v] * pps))
    return out.reshape(bsz * t, BRANCH_WIDTH)


RW_LORA_OFF = 3 * BRANCH_WIDTH
RW_G_OFF = RW_LORA_OFF + LANES


def _rwkv_prep_kernel(x_ref, prev_ref, shift_ref, mu_ref, w0_ref, w2_ref, a0_ref, a2_ref, g2_ref,
                      kkw_ref, ka_ref, rk_ref, e_ref,
                      a_out, q_out, w_out, b_out, k_out, vkr_out, v_out, bon_out, g_out):
    bw = BRANCH_WIDTH
    x = x_ref[...]
    tm = x.shape[0]
    first = pl.program_id(1) == 0
    prev_row = jnp.where(first, shift_ref[...], prev_ref[SUBLANES - 1:SUBLANES, :])
    row = lax.broadcasted_iota(jnp.int32, x.shape, 0)
    prev = jnp.where(row == 0, prev_row, pltpu.roll(x, 1, axis=0))
    pm = x + (prev - x) * mu_ref[...]
    r = pm[:, :bw]
    k = pm[:, bw:2 * bw]
    v = pm[:, 2 * bw:3 * bw]
    lora = pm[:, RW_LORA_OFF:RW_LORA_OFF + LANES]
    glo = pm[:, RW_G_OFF:RW_G_OFF + 2 * LANES]
    w = -_softplus(-(w0_ref[...] + _dot(jnp.tanh(lora).astype(BF16), w2_ref[...]))) - 0.5
    decay = jnp.exp(-jnp.exp(w))
    a = jax.nn.sigmoid(a0_ref[...] + _dot(lora.astype(BF16), a2_ref[...]))
    g = _dot(jax.nn.sigmoid(glo).astype(BF16), g2_ref[...])
    kk = k * kkw_ref[...]
    n2 = _segsum_bcast(kk * kk, e_ref, LANES)
    kk = kk / jnp.maximum(jnp.sqrt(n2), 1e-12)
    k2 = k * (1.0 + (a - 1.0) * ka_ref[...])
    bm = kk * a
    a_out[...] = -kk
    q_out[...] = decay * r - kk * _segsum_bcast(bm * r, e_ref, LANES)
    w_out[...] = decay
    b_out[...] = bm
    k_out[...] = k2
    vkr_out[...] = v * _segsum_bcast(k2 * r, e_ref, LANES)
    v_out[...] = v
    bon_out[...] = _segsum_bcast(r * k2 * rk_ref[...], e_ref, LANES) * v
    g_out[...] = g


def _rwkv_prep(rw, shift0, P, bsz, t, tm):
    n = rw.shape[0]
    bw = BRANCH_WIDTH
    cp = RWKV_COLS_PAD
    tiles = t // tm
    vec = lambda nm: pl.BlockSpec((1, bw), lambda b, i: (0, 0))
    out_spec = pl.BlockSpec((tm, bw), lambda b, i: (b * tiles + i, 0))
    rows8 = tm // SUBLANES
    outs = pl.pallas_call(
        _rwkv_prep_kernel,
        grid=(bsz, tiles),
        in_specs=[
            pl.BlockSpec((tm, cp), lambda b, i: (b * tiles + i, 0)),
            pl.BlockSpec((SUBLANES, cp), lambda b, i: (jnp.maximum((b * tiles + i) * rows8 - 1, 0), 0)),
            pl.BlockSpec((None, 1, cp), lambda b, i: (b, 0, 0)),
            pl.BlockSpec((1, cp), lambda b, i: (0, 0)),
            vec("w0"),
            pl.BlockSpec((LANES, bw), lambda b, i: (0, 0)),
            vec("a0"),
            pl.BlockSpec((LANES, bw), lambda b, i: (0, 0)),
            pl.BlockSpec((2 * LANES, bw), lambda b, i: (0, 0)),
            vec("kk"), vec("ka"), vec("rk"),
            pl.BlockSpec((LANES, LANES), lambda b, i: (0, 0)),
        ],
        out_specs=[out_spec] * 9,
        out_shape=[jax.ShapeDtypeStruct((n, bw), F32)] * 9,
        compiler_params=_cparams(("arbitrary", "arbitrary")),
        name="rwkv_prep",
    )(rw, rw, shift0.reshape(bsz, 1, cp), P["mu"], P["w0"], P["w2"], P["a0"], P["a2"], P["g2"],
      P["kk"], P["ka"], P["rk"], _block_ones(LANES, RWKV_DH))
    return outs


def _rwkv_scan_kernel(a_ref, q_ref, w_ref, b_ref, k_ref, vkr_ref, v_ref, bon_ref, g_ref,
                      s0_ref, lng_ref, lnb_ref, e_ref, o_ref, sT_ref, s_scr, y_scr, *, bb, tc, pg, cpd):
    c = pl.program_id(2)

    @pl.when(c == 0)
    def _():
        s_scr[...] = s0_ref[...]

    e = e_ref[...]
    sub = lax.broadcasted_iota(jnp.int32, (RWKV_DH, LANES), 0)
    lane = lax.broadcasted_iota(jnp.int32, (RWKV_DH, LANES), 1)
    diag = (lane % RWKV_DH) == sub

    chains = [(b, p) for b in range(bb) for p in range(pg)]

    def group(tg, carry):
        t0 = pl.multiple_of(tg * SUBLANES, SUBLANES)
        states = [s_scr[b, p] for b, p in chains]
        ys = [[] for _ in chains]
        for r in range(SUBLANES):
            for c0 in range(0, len(chains), cpd):
                group_chains = list(enumerate(chains))[c0:c0 + cpd]
                row = lambda ref, b, p: ref[b, pl.ds(t0, SUBLANES), pl.ds(p * LANES, LANES)][r:r + 1, :]
                parts = []
                for ci, (b, p) in group_chains:
                    s = states[ci]
                    parts += [(s * row(a_ref, b, p)).astype(BF16), (s * row(q_ref, b, p)).astype(BF16),
                              jnp.where(diag, row(v_ref, b, p), 0.0).astype(BF16)]
                res = _dot(jnp.concatenate(parts, axis=0), e)
                for gi, (ci, (b, p)) in enumerate(group_chains):
                    base = 3 * RWKV_DH * gi
                    sa = res[base:base + RWKV_DH]
                    m2 = res[base + RWKV_DH:base + 2 * RWKV_DH]
                    vb = res[base + 2 * RWKV_DH:base + 3 * RWKV_DH]
                    states[ci] = states[ci] * row(w_ref, b, p) + sa * row(b_ref, b, p) + vb * row(k_ref, b, p)
                    ys[ci].append(jnp.sum(jnp.where(diag, m2, 0.0), axis=0, keepdims=True) + row(vkr_ref, b, p))
        for ci, (b, p) in enumerate(chains):
            s_scr[b, p] = states[ci]
            y_scr[b, pl.ds(t0, SUBLANES), pl.ds(p * LANES, LANES)] = jnp.concatenate(ys[ci], axis=0)
        return carry

    lax.fori_loop(0, tc // SUBLANES, group, 0)

    for b in range(bb):
        y = y_scr[b]
        mu = _segsum_bcast(y, e_ref, LANES) * (1.0 / RWKV_DH)
        d = y - mu
        var = _segsum_bcast(d * d, e_ref, LANES) * (1.0 / RWKV_DH)
        yn = d * lax.rsqrt(var + RWKV_LN_EPS) * lng_ref[...] + lnb_ref[...]
        o_ref[b] = ((yn + bon_ref[b]) * g_ref[b]).astype(o_ref.dtype)

    @pl.when(c == pl.num_programs(2) - 1)
    def _():
        sT_ref[...] = s_scr[...]


def _rwkv_scan(prep, s0, lng, lnb, bsz, t, bb, tc, pg, cpd=2):
    bw = BRANCH_WIDTH
    npair = RWKV_HEADS // 2
    assert tc % SUBLANES == 0 and npair % pg == 0
    wcols = pg * LANES
    seq = [z.reshape(bsz, t, bw) for z in prep]
    blk = pl.BlockSpec((bb, tc, wcols), lambda b, q, c: (b, c, q))
    st = pl.BlockSpec((bb, pg, RWKV_DH, LANES), lambda b, q, c: (b, q, 0, 0))
    vec = pl.BlockSpec((1, wcols), lambda b, q, c: (0, q))
    assert (bb * pg) % cpd == 0
    kern = functools.partial(_rwkv_scan_kernel, bb=bb, tc=tc, pg=pg, cpd=cpd)
    o, s_t = pl.pallas_call(
        kern,
        grid=(bsz // bb, npair // pg, t // tc),
        in_specs=[blk] * 9 + [st, vec, vec, pl.BlockSpec((LANES, LANES), lambda b, q, c: (0, 0))],
        out_specs=[blk, st],
        out_shape=[jax.ShapeDtypeStruct((bsz, t, bw), BF16),
                   jax.ShapeDtypeStruct((bsz, npair, RWKV_DH, LANES), F32)],
        scratch_shapes=[pltpu.VMEM((bb, pg, RWKV_DH, LANES), F32), pltpu.VMEM((bb, tc, wcols), F32)],
        compiler_params=_cparams(("arbitrary", "arbitrary", "arbitrary")),
        name="rwkv_scan",
    )(*seq, s0, lng, lnb, _block_ones(LANES, RWKV_DH))
    return o.reshape(bsz * t, bw), s_t


def _mamba_kernel(x_ref, conv0_ref, h0_ref, cw_ref, cb_ref, dtb_ref, alog_ref, dvec_ref, nrm_ref,
                  tri_ref, eye_ref, xp_ref, xpt_ref, e256_ref, o_ref, hT_ref, tail_scr, h_scr, *, ln):
    bw = BRANCH_WIDTH
    gn = SSM_GROUPS * SSM_STATE
    npair = SSM_HEADS // 2
    c = pl.program_id(1)

    @pl.when(c == 0)
    def _():
        tail_scr[...] = conv0_ref[...]
        h_scr[...] = h0_ref[...]

    blk = x_ref[...]
    z = blk[:, :bw]
    xbc = blk[:, bw:bw + SSM_CONV_DIM]
    dtr = blk[:, bw + SSM_CONV_DIM:bw + SSM_CONV_DIM + LANES]
    xpad = jnp.concatenate([tail_scr[...], xbc], axis=0)
    conv = cb_ref[...]
    for i in range(SSM_CONV):
        off = SUBLANES - (SSM_CONV - 1) + i
        conv = conv + xpad[off:off + ln] * cw_ref[i:i + 1, :]
    tail_scr[...] = xpad[ln:ln + SUBLANES]
    conv = _silu(conv)
    xs = conv[:, :bw]
    lane = lax.broadcasted_iota(jnp.int32, (ln, LANES), 1)
    dt = jnp.where(lane < SSM_HEADS, _softplus(dtr + dtb_ref[...]), 0.0)
    a_neg = -jnp.exp(alog_ref[...])
    a = dt * a_neg
    ah, am_, al = _split3(a)
    tri = tri_ref[...]
    a_cs = _dot(tri, ah) + _dot(tri, am_) + _dot(tri, al)
    ch, cm, cl = _split3(a_cs)
    eye = eye_ref[...]
    a_cs_t = _dot_nt(eye, ch) + _dot_nt(eye, cm) + _dot_nt(eye, cl)
    a_tot = a_cs[ln - 1:ln, :]
    xp = xp_ref[...]
    dt_x = _dot_x3(dt, xp)
    acs_x = _dot_x3(a_cs, xp)
    atot_x = _dot_x3(jnp.broadcast_to(a_tot, (SUBLANES, LANES)), xp)[0:1]
    th, tm_, tl = _split3(jnp.broadcast_to(a_tot, (SUBLANES, LANES)))
    xpt = xpt_ref[...]
    atot_col = (_dot_nt(xpt, th) + _dot_nt(xpt, tm_) + _dot_nt(xpt, tl))[:, 0:1]
    xc = xs * dt_x
    xd = xc * jnp.exp(atot_x - acs_x)
    row_i = lax.broadcasted_iota(jnp.int32, (ln, ln), 0)
    col_i = lax.broadcasted_iota(jnp.int32, (ln, ln), 1)
    tril = row_i >= col_i
    lane_h = lax.broadcasted_iota(jnp.int32, (ln, LANES), 1) < SSM_HEAD_DIM
    y_parts = []
    for g in range(SSM_GROUPS):
        bg = conv[:, bw + g * SSM_STATE:bw + (g + 1) * SSM_STATE].astype(BF16)
        cg = conv[:, bw + gn + g * SSM_STATE:bw + gn + (g + 1) * SSM_STATE].astype(BF16)
        cbm = _dot_nt(cg, bg)
        for pr in range(2):
            pi = g * 2 + pr
            cs = slice(pi * LANES, (pi + 1) * LANES)
            xcp = xc[:, cs]
            y = None
            for e in range(2):
                hd = 2 * pi + e
                seg = a_cs[:, hd:hd + 1] - a_cs_t[hd:hd + 1, :]
                wm = jnp.where(tril, cbm * jnp.exp(jnp.where(tril, seg, 0.0)), 0.0).astype(BF16)
                xm = jnp.where(lane_h if e == 0 else jnp.logical_not(lane_h), xcp, 0.0).astype(BF16)
                part = _dot(wm, xm)
                y = part if y is None else y + part
            hp = h_scr[pi]
            y = y + _dot_nt(cg, hp.astype(BF16)) * jnp.exp(acs_x[:, cs])
            y_parts.append(y)
            h_new = jnp.exp(atot_col[pi * LANES:(pi + 1) * LANES, :]) * hp + _dot_tn(xd[:, cs].astype(BF16), bg)
            h_scr[pi] = h_new
    y = jnp.concatenate(y_parts, axis=1)
    y = (y + dvec_ref[...] * xs) * _silu(z)
    gw = bw // SSM_GROUPS
    ms = _segsum_bcast(y * y, e256_ref, gw) * (1.0 / gw)
    y = y * lax.rsqrt(ms + 1e-5) * nrm_ref[...]
    o_ref[...] = y.astype(o_ref.dtype)

    @pl.when(c == pl.num_programs(1) - 1)
    def _():
        hT_ref[...] = h_scr[...]


def _mamba(sm, conv0, h0, P, bsz, t):
    bw = BRANCH_WIDTH
    ln = min(SSM_CHUNK, t)
    assert t % ln == 0
    nc = t // ln
    npair = SSM_HEADS // 2
    tri = jnp.asarray(np.tril(np.ones((ln, ln), np.float32)), dtype=BF16)
    eye = jnp.asarray(np.eye(LANES, dtype=np.float32), dtype=BF16)
    xp_np = (np.arange(LANES)[:, None] == (np.arange(bw)[None, :] // SSM_HEAD_DIM)).astype(np.float32)
    xp = jnp.asarray(xp_np, dtype=BF16)
    xpt = jnp.asarray(xp_np.T.copy(), dtype=BF16)
    const = lambda shape: pl.BlockSpec(shape, lambda b, c: tuple(0 for _ in shape))
    kern = functools.partial(_mamba_kernel, ln=ln)
    o, h_t = pl.pallas_call(
        kern,
        grid=(bsz, nc),
        in_specs=[
            pl.BlockSpec((ln, SSM_COLS_PAD), lambda b, c: (b * nc + c, 0)),
            pl.BlockSpec((None, SUBLANES, SSM_CONV_DIM), lambda b, c: (b, 0, 0)),
            pl.BlockSpec((None, npair, LANES, SSM_STATE), lambda b, c: (b, 0, 0, 0)),
            const((SSM_CONV, SSM_CONV_DIM)), const((1, SSM_CONV_DIM)), const((1, LANES)), const((1, LANES)),
            const((1, bw)), const((1, bw)), const((ln, ln)), const((LANES, LANES)),
            const((LANES, bw)), const((bw, LANES)), const((bw // SSM_GROUPS, bw // SSM_GROUPS)),
        ],
        out_specs=[pl.BlockSpec((ln, bw), lambda b, c: (b * nc + c, 0)),
                   pl.BlockSpec((None, npair, LANES, SSM_STATE), lambda b, c: (b, 0, 0, 0))],
        out_shape=[jax.ShapeDtypeStruct((bsz * t, bw), BF16),
                   jax.ShapeDtypeStruct((bsz, npair, LANES, SSM_STATE), F32)],
        scratch_shapes=[pltpu.VMEM((SUBLANES, SSM_CONV_DIM), F32), pltpu.VMEM((npair, LANES, SSM_STATE), F32)],
        compiler_params=_cparams(("arbitrary", "arbitrary")),
        name="mamba",
    )(sm, conv0, h0, P["cw"], P["cb"], P["dtb"], P["alog"], P["dvec"], P["nrm"], tri, eye, xp, xpt,
      _block_ones(bw // SSM_GROUPS, bw // SSM_GROUPS))
    return o, h_t


PEER_NO_RANK = float(PEER_NKEYS)


def _topk_rows(s, k, with_rank=False):
    rows = s.shape[0]
    iota = lax.broadcasted_iota(jnp.int32, s.shape, 0)
    rank = jnp.full(s.shape, PEER_NO_RANK, F32)
    outs = []
    for i in range(k):
        m = jnp.max(s, axis=0, keepdims=True)
        outs.append(m)
        idx = jnp.min(jnp.where(s == m, iota, rows), axis=0, keepdims=True)
        hit = iota == idx
        s = jnp.where(hit, -jnp.inf, s)
        if with_rank:
            rank = jnp.where(hit, float(i), rank)
    top = jnp.concatenate(outs, axis=0)
    return (top, rank) if with_rank else top


def _peer_route_kernel(q_ref, k1_ref, k2_ref, cnt_ref, rank2_ref, e1_ref, e2_ref):
    sc = []
    for half, k_ref in enumerate((k1_ref, k2_ref)):
        kh, kl = _split2(k_ref[...])
        qh, ql = _split2(q_ref[:, half * PEER_HALF:(half + 1) * PEER_HALF])
        sc.append(_dot_nt(kh, qh) + _dot_nt(kh, ql) + _dot_nt(kl, qh))
    v1, rank1 = _topk_rows(sc[0], PEER_TOPK, with_rank=True)
    v2, rank2 = _topk_rows(sc[1], PEER_TOPK, with_rank=True)
    row8 = lax.broadcasted_iota(jnp.int32, (SUBLANES, v2.shape[1]), 0)
    pieces = [v1[0:1, :] + v2]
    for i in range(1, SUBLANES):
        n_i = PEER_TOPK // (i + 1)
        pieces.append(jnp.where(row8 < n_i, v1[i:i + 1, :] + v2[0:SUBLANES, :], -jnp.inf))
    pieces.append(v1[SUBLANES:PEER_TOPK, :] + v2[0:1, :])
    cand = jnp.concatenate(pieces, axis=0)
    top = _topk_rows(cand, PEER_TOPK)
    tau = top[PEER_TOPK - 1:PEER_TOPK, :]
    zsum = jnp.sum(jnp.exp(top - top[0:1, :]), axis=0, keepdims=True)
    cnt = jnp.zeros(rank1.shape, F32)
    for i in range(PEER_TOPK):
        c_i = jnp.sum(((v1[i:i + 1, :] + v2) >= tau).astype(F32), axis=0, keepdims=True)
        cnt = jnp.where(rank1 == float(i), c_i, cnt)
    cnt_ref[...] = cnt
    rank2_ref[...] = rank2.astype(BF16)
    e1_ref[...] = jnp.exp(sc[0] - v1[0:1, :]) / zsum
    e2_ref[...] = jnp.exp(sc[1] - v2[0:1, :]).astype(BF16)


def _peer_route(q, k1, k2, tt):
    n = q.shape[0]
    arr = lambda dt: jax.ShapeDtypeStruct((PEER_HEADS, PEER_NKEYS, n), dt)
    spec = pl.BlockSpec((None, PEER_NKEYS, tt), lambda i, h: (h, 0, i))
    return pl.pallas_call(
        _peer_route_kernel,
        grid=(n // tt, PEER_HEADS),
        in_specs=[pl.BlockSpec((tt, PEER_KEY_DIM), lambda i, h: (i, h)),
                  pl.BlockSpec((PEER_NKEYS, PEER_HALF), lambda i, h: (0, 0)),
                  pl.BlockSpec((PEER_NKEYS, PEER_HALF), lambda i, h: (0, 0))],
        out_specs=[spec, spec, spec, spec],
        out_shape=[arr(F32), arr(BF16), arr(F32), arr(BF16)],
        compiler_params=_cparams(("arbitrary", "arbitrary")),
        name="peer_route",
    )(q, k1, k2)


def _peer_dense_kernel(ht_ref, u_ref, vt_ref, cnt_ref, rank2_ref, e1_ref, e2_ref, o_ref, act_scr, ca_scr,
                       *, ib, nsplit):
    j = pl.program_id(1)

    @pl.when(j == 0)
    def _():
        o_ref[...] = jnp.zeros(o_ref.shape, F32)

    ht = ht_ref[...]
    per = ib // nsplit
    rows = per * PEER_NKEYS
    for part in range(nsplit):
        act_scr[part * rows:(part + 1) * rows, :] = _dot(u_ref[part * rows:(part + 1) * rows, :], ht)
    for part in range(nsplit):
        for ii in range(part * per, (part + 1) * per):
            sl = slice(ii * PEER_NKEYS, (ii + 1) * PEER_NKEYS)
            act = act_scr[sl, :]
            act = 0.5 * act * (1.0 + lax.erf(act * (2.0 ** -0.5)))
            coef = jnp.zeros(act.shape, BF16)
            for h in range(PEER_HEADS):
                keep = rank2_ref[h] < cnt_ref[h, ii:ii + 1, :].astype(BF16)
                coef = coef + jnp.where(keep, e2_ref[h], 0.0) * e1_ref[h, ii:ii + 1, :].astype(BF16)
            ca_scr[sl, :] = coef * act.astype(BF16)
        o_ref[...] += _dot(vt_ref[:, part * rows:(part + 1) * rows], ca_scr[part * rows:(part + 1) * rows, :])


def _peer_dense(ht, u, vt, l, cnt, rank2, e1, e2, tt, ib, nsplit=2):
    d, n = ht.shape
    eb = ib * PEER_NKEYS
    kern = functools.partial(_peer_dense_kernel, ib=ib, nsplit=nsplit)
    full = pl.BlockSpec((PEER_HEADS, PEER_NKEYS, tt), lambda i, j: (0, 0, i))
    part = pl.BlockSpec((PEER_HEADS, ib, tt), lambda i, j: (0, j, i))
    return pl.pallas_call(
        kern,
        grid=(n // tt, PEER_EXPERTS // eb),
        in_specs=[
            pl.BlockSpec((d, tt), lambda i, j: (0, i)),
            pl.BlockSpec((None, eb, d), lambda i, j: (l, j, 0)),
            pl.BlockSpec((None, d, eb), lambda i, j: (l, 0, j)),
            part, full, part, full,
        ],
        out_specs=pl.BlockSpec((d, tt), lambda i, j: (0, i)),
        out_shape=jax.ShapeDtypeStruct((d, n), F32),
        scratch_shapes=[pltpu.VMEM((eb, tt), F32), pltpu.VMEM((eb, tt), BF16)],
        compiler_params=_cparams(("arbitrary", "arbitrary")),
        name="peer_dense",
    )(ht, u, vt, cnt, rank2, e1, e2)


def _peer_res_kernel(x_ref, ot_ref, g_ref, o_ref):
    o_ref[...] = x_ref[...] + g_ref[...] * ot_ref[...].T


def _peer_res(x, out_t, mod, g_blk, tm):
    n, d = x.shape
    nseq, r, _ = mod.shape
    tiles_per_seq = (n // tm) // nseq
    return pl.pallas_call(
        _peer_res_kernel,
        grid=(n // tm,),
        in_specs=[pl.BlockSpec((tm, d), lambda i: (i, 0)),
                  pl.BlockSpec((d, tm), lambda i: (0, i)),
                  pl.BlockSpec((None, r, d), lambda i: (i // tiles_per_seq, 0, g_blk))],
        out_specs=pl.BlockSpec((tm, d), lambda i: (i, 0)),
        out_shape=jax.ShapeDtypeStruct((n, d), F32),
        compiler_params=_cparams(("arbitrary",)),
        name="peer_res",
    )(x, out_t, mod)


def _pad_cols(a, width):
    return jnp.pad(a, [(0, 0)] * (a.ndim - 1) + [(0, width - a.shape[-1])])


def _pad_rows(a, rows):
    return jnp.pad(a, [(0, rows - a.shape[0])] + [(0, 0)] * (a.ndim - 1))


def _big_weights(W):
    w_in = W["w_in"]
    return {
        "w_qkv": w_in[:, :, :RWKV_OFF].astype(BF16),
        "w_rw": _pad_cols(w_in[:, :, RWKV_OFF:SSM_OFF], RWKV_COLS_PAD).astype(BF16),
        "w_sm": _pad_cols(w_in[:, :, SSM_OFF:GATE_OFF], SSM_COLS_PAD).astype(BF16),
        "w_gate": w_in[:, :, GATE_OFF:].astype(BF16),
        "wb": W["w_branch"].astype(BF16),
        "wo": W["w_o"].astype(BF16),
        "wq": W["peer_wq"].astype(BF16),
        "u": W["peer_u"].astype(BF16),
        "vt": jnp.swapaxes(W["peer_v"].astype(BF16), 1, 2),
    }


def _layer_weights(l, W, big):
    bw = BRANCH_WIDTH
    out = dict(big)
    o1 = 3 * bw
    o2 = o1 + RWKV_DECAY_LORA
    o3 = o2 + RWKV_AAA_LORA
    rwp = {
        "mu": _pad_cols(W["rwkv_mu"][l][None, :], RWKV_COLS_PAD),
        "w0": W["rwkv_w0"][l][None, :],
        "w2": _pad_rows(W["rwkv_w2"][l], LANES).astype(BF16),
        "a0": W["rwkv_a0"][l][None, :],
        "a2": jnp.concatenate([jnp.zeros((RWKV_DECAY_LORA, bw), F32), W["rwkv_a2"][l]], axis=0).astype(BF16),
        "g2": _pad_rows(W["rwkv_g2"][l], 2 * LANES).astype(BF16),
        "kk": W["rwkv_kk"][l][None, :],
        "ka": W["rwkv_ka"][l][None, :],
        "rk": W["rwkv_rk"][l].reshape(1, bw),
        "lng": W["rwkv_lnx_g"][l][None, :],
        "lnb": W["rwkv_lnx_b"][l][None, :],
    }
    smp = {
        "cw": W["ssm_conv_w"][l],
        "cb": W["ssm_conv_b"][l][None, :],
        "dtb": _pad_cols(W["ssm_dt_bias"][l][None, :], LANES),
        "alog": _pad_cols(W["ssm_a_log"][l][None, :], LANES),
        "dvec": jnp.repeat(W["ssm_d"][l], SSM_HEAD_DIM)[None, :],
        "nrm": W["ssm_norm"][l][None, :],
    }
    lv = W["att_lambda"][l].astype(F32)
    lam_init = 0.8 - 0.6 * math.exp(-0.3 * l)
    lam = jnp.exp(jnp.sum(lv[0] * lv[1])) - jnp.exp(jnp.sum(lv[2] * lv[3])) + lam_init
    out.update(rwp=rwp, smp=smp, lam=lam, lam_init=lam_init)
    return out


def _layer(l, x, mod, lw, W, bsz, t, state, cfg, paged):
    n = bsz * t
    tm, tn = cfg["tm"], cfg["tn"]
    shift0, wkv0, conv0, ssm0 = state
    h = _norm(x, W["norm1"][l], mod, 1, 0, cfg["tm_n"])[0]
    mm = lambda w, tn_, dt=F32: _mm(h, w, l, tm, tn_, dt)
    qkv = mm(lw["w_qkv"], 512)
    rw = mm(lw["w_rw"], 384)
    sm = mm(lw["w_sm"], 640)
    gates = mm(lw["w_gate"], 512, BF16)
    if paged is None:
        o_att = _attn_prompt(qkv, W["rel_bias"], lw["lam"], W["att_subln"][l], lw["lam_init"], bsz, t, cfg["tq"])
    else:
        cache_k, cache_v, page_table = paged
        o_att = _attn_sample(qkv, cache_k, cache_v, page_table, l, W["rel_bias"], lw["lam"], W["att_subln"][l],
                             lw["lam_init"], bsz, t)
    prep = _rwkv_prep(rw, shift0, lw["rwp"], bsz, t, cfg["rw_tm"])
    o_rwkv, wkv1 = _rwkv_scan(prep, wkv0, lw["rwp"]["lng"], lw["rwp"]["lnb"], bsz, t, cfg["rw_bb"], cfg["rw_tc"],
                               cfg["rw_pg"])
    o_ssm, ssm1 = _mamba(sm, conv0, ssm0, lw["smp"], bsz, t)
    merged = _merge(o_att, o_rwkv, o_ssm, gates, lw["wb"], l, cfg["tm2"], tn)
    x = _proj_res(merged, lw["wo"], l, x, mod, 2, cfg["tm2"], tn)
    npad = -(-n // LANES) * LANES
    xq, modq, tmq = x, mod, tm
    if npad != n:
        xq = _pad_rows(x, npad)
        modq = jnp.pad(mod, ((0, 0), (0, npad - n), (0, 0)))
        tmq = npad
    h2, ht = _norm(xq, W["norm2"][l], modq, 4, 3, cfg["tmT"], with_transpose=True)
    q = _mm(h2, lw["wq"], l, tmq, tn)
    cnt, rank2, e1, e2 = _peer_route(q, W["peer_k1"][l], W["peer_k2"][l], cfg["tt_route"])
    out_t = _peer_dense(ht, lw["u"], lw["vt"], l, cnt, rank2, e1, e2, cfg["tt_dense"], cfg["ib"])
    x = _peer_res(xq, out_t, modq, 5, cfg["tmT"])[:n]
    qkv3 = qkv.reshape(bsz, t, 3 * ATT_QK)
    k_new = qkv3[:, :, ATT_QK:2 * ATT_QK].reshape(bsz, t, ATT_HEADS, 2 * ATT_DH)
    v_new = qkv3[:, :, 2 * ATT_QK:].reshape(bsz, t, ATT_HEADS, ATT_VD)
    shift1 = rw.reshape(bsz, t, RWKV_COLS_PAD)[:, -1, :RWKV_COLS]
    assert t >= SSM_CONV - 1
    conv1 = sm.reshape(bsz, t, SSM_COLS_PAD)[:, -(SSM_CONV - 1):, BRANCH_WIDTH:BRANCH_WIDTH + SSM_CONV_DIM]
    return x, (k_new, v_new, shift1, wkv1, conv1, ssm1)


def _wkv_to_kernel(s):
    b = s.shape[0]
    s = s.reshape(b, RWKV_HEADS // 2, 2, RWKV_DH, RWKV_DH)
    return jnp.transpose(s, (0, 1, 3, 2, 4)).reshape(b, RWKV_HEADS // 2, RWKV_DH, 2 * RWKV_DH)


def _wkv_from_kernel(s):
    b = s.shape[0]
    s = s.reshape(b, RWKV_HEADS // 2, RWKV_DH, 2, RWKV_DH)
    return jnp.transpose(s, (0, 1, 3, 2, 4)).reshape(b, RWKV_HEADS, RWKV_DH, RWKV_DH)


def _run_group(x, mod_of_layer, W, lws, bsz, t, states, cfg, paged_of_layer):
    xs = x.reshape(bsz * t, D_MODEL)
    outs = []
    for l in range(DEPTH):
        xs, new = _layer(l, xs, mod_of_layer(l), lws[l], W, bsz, t, states[l], cfg, paged_of_layer(l))
        outs.append(new)
    y = _final_norm(xs, W["norm_f"], cfg["tm_f"]).reshape(bsz, t, D_MODEL)
    k = jnp.stack([o[0] for o in outs], axis=1)
    v = jnp.stack([o[1] for o in outs], axis=1)
    shift = jnp.stack([o[2] for o in outs], axis=0)
    wkv = jnp.stack([_wkv_from_kernel(o[3]) for o in outs], axis=0)
    conv = jnp.stack([o[4] for o in outs], axis=0)
    ssm = jnp.stack([o[5].reshape(bsz, SSM_HEADS, SSM_HEAD_DIM, SSM_STATE) for o in outs], axis=0)
    return y, k, v, shift, wkv, conv, ssm


PROMPT_CFG = dict(tm=2048, tm_n=256, tn=512, tm2=512, tq=512, rw_tm=256, rw_bb=2, rw_tc=128, rw_pg=8, tmT=256,
                  tt_route=512, tt_dense=512, ib=8, tm_f=512)


def _sample_cfg(n, t):
    npad = -(-n // LANES) * LANES
    return dict(tm=n, tm_n=n, tn=512, tm2=n, tq=None, rw_tm=t, rw_bb=2, rw_tc=t, rw_pg=4, tmT=npad,
                tt_route=npad, tt_dense=npad, ib=8, tm_f=n)


def kernel(x_prompt, x_sample, cache_k, cache_v, state_rwkv_shift, state_rwkv, state_conv, state_ssm, page_table, c_prompt, c_sample, w_ada, b_ada, norm1, norm2, norm_f, w_in, rel_bias, att_lambda, att_subln, rwkv_mu, rwkv_w0, rwkv_w2, rwkv_a0, rwkv_a2, rwkv_g2, rwkv_kk, rwkv_ka, rwkv_rk, rwkv_lnx_g, rwkv_lnx_b, ssm_conv_w, ssm_conv_b, ssm_dt_bias, ssm_a_log, ssm_d, ssm_norm, w_branch, w_o, peer_wq, peer_k1, peer_k2, peer_u, peer_v):
    W = dict(norm1=norm1, norm2=norm2, norm_f=norm_f, w_in=w_in, rel_bias=rel_bias, att_lambda=att_lambda,
             att_subln=att_subln, rwkv_mu=rwkv_mu, rwkv_w0=rwkv_w0, rwkv_w2=rwkv_w2, rwkv_a0=rwkv_a0,
             rwkv_a2=rwkv_a2, rwkv_g2=rwkv_g2, rwkv_kk=rwkv_kk, rwkv_ka=rwkv_ka, rwkv_rk=rwkv_rk,
             rwkv_lnx_g=rwkv_lnx_g, rwkv_lnx_b=rwkv_lnx_b, ssm_conv_w=ssm_conv_w, ssm_conv_b=ssm_conv_b,
             ssm_dt_bias=ssm_dt_bias, ssm_a_log=ssm_a_log, ssm_d=ssm_d, ssm_norm=ssm_norm, w_branch=w_branch,
             w_o=w_o, peer_wq=peer_wq, peer_k1=peer_k1, peer_k2=peer_k2, peer_u=peer_u, peer_v=peer_v)
    bp, tp, _ = x_prompt.shape
    bs, ts, _ = x_sample.shape
    nseq = bp + bs
    rows = -(-nseq // SUBLANES) * SUBLANES
    c_all = _pad_rows(jnp.concatenate([c_prompt, c_sample], axis=0), rows)
    mod = _ada(c_all, w_ada, b_ada)
    big = _big_weights(W)
    lws = [_layer_weights(l, W, big) for l in range(DEPTH)]
    npair = RWKV_HEADS // 2
    zero_state = (jnp.zeros((bp, RWKV_COLS_PAD), F32),
                  jnp.zeros((bp, npair, RWKV_DH, LANES), F32),
                  jnp.zeros((bp, SUBLANES, SSM_CONV_DIM), F32),
                  jnp.zeros((bp, SSM_HEADS // 2, LANES, SSM_STATE), F32))
    outs_p = _run_group(x_prompt, lambda l: mod[l, :bp].reshape(bp, 1, 6 * D_MODEL), W, lws, bp, tp,
                        [zero_state] * DEPTH, PROMPT_CFG, lambda l: None)
    states_s = []
    for l in range(DEPTH):
        states_s.append((
            _pad_cols(state_rwkv_shift[l], RWKV_COLS_PAD),
            _wkv_to_kernel(state_rwkv[l]),
            jnp.pad(state_conv[l], ((0, 0), (SUBLANES - (SSM_CONV - 1), 0), (0, 0))),
            state_ssm[l].reshape(bs, SSM_HEADS // 2, LANES, SSM_STATE),
        ))
    ns = bs * ts
    mod_s = lambda l: jnp.repeat(mod[l, bp:bp + bs], ts, axis=0).reshape(1, ns, 6 * D_MODEL)
    outs_s = _run_group(x_sample, mod_s, W, lws, bs, ts, states_s, _sample_cfg(ns, ts),
                        lambda l: (cache_k, cache_v, page_table))
    return (outs_p[0], outs_s[0], *outs_p[1:], *outs_s[1:])
```

```python
import functools
import math

import jax
import jax.numpy as jnp
import numpy as np
from jax import lax
from jax.experimental import pallas as pl
from jax.experimental.pallas import tpu as pltpu

F32 = jnp.float32
BF16 = jnp.bfloat16

LANES = 128
SUBLANES = 8
VMEM_LIMIT_BYTES = 56 * 1024 * 1024

D_MODEL = 2048
DEPTH = 4
PAGE_SIZE = 128
BRANCH_WIDTH = D_MODEL // 2
N_BRANCHES = 3
ATT_DH = 64
ATT_VD = 2 * ATT_DH
ATT_HEADS = BRANCH_WIDTH // ATT_VD
ATT_QK = ATT_HEADS * 2 * ATT_DH
REL_BUCKETS = 32
REL_MAX_DIST = 128
RWKV_DH = 64
RWKV_HEADS = BRANCH_WIDTH // RWKV_DH
RWKV_DECAY_LORA = 64
RWKV_AAA_LORA = 64
RWKV_GATE_LORA = 160
RWKV_LN_EPS = 64e-5
RWKV_COLS = 3 * BRANCH_WIDTH + RWKV_DECAY_LORA + RWKV_AAA_LORA + RWKV_GATE_LORA
RWKV_COLS_PAD = 3456
SSM_HEAD_DIM = 64
SSM_HEADS = BRANCH_WIDTH // SSM_HEAD_DIM
SSM_GROUPS = 4
SSM_STATE = 128
SSM_CONV = 4
SSM_CONV_DIM = BRANCH_WIDTH + 2 * SSM_GROUPS * SSM_STATE
SSM_CHUNK = 128
SSM_COLS = BRANCH_WIDTH + SSM_CONV_DIM + SSM_HEADS
SSM_COLS_PAD = 3200
PEER_HEADS = 8
PEER_NKEYS = 128
PEER_EXPERTS = PEER_NKEYS * PEER_NKEYS
PEER_KEY_DIM = 256
PEER_HALF = PEER_KEY_DIM // 2
PEER_TOPK = 16
ATT_Q_OFF = 0
ATT_K_OFF = ATT_QK
ATT_V_OFF = 2 * ATT_QK
RWKV_OFF = ATT_V_OFF + ATT_HEADS * ATT_VD
SSM_OFF = RWKV_OFF + RWKV_COLS
GATE_OFF = SSM_OFF + SSM_COLS

NEG_BIG = -1e30


def _cparams(semantics):
    return pltpu.CompilerParams(dimension_semantics=semantics, vmem_limit_bytes=VMEM_LIMIT_BYTES)


def _dot(a, b):
    return jnp.dot(a, b, preferred_element_type=F32)


def _dot_nt(a, b):
    return lax.dot_general(a, b, (((1,), (1,)), ((), ())), preferred_element_type=F32)


def _dot_tn(a, b):
    return lax.dot_general(a, b, (((0,), (0,)), ((), ())), preferred_element_type=F32)


def _split2(x):
    hi = x.astype(BF16)
    lo = (x - hi.astype(F32)).astype(BF16)
    return hi, lo


def _split3(x):
    hi = x.astype(BF16)
    r = x - hi.astype(F32)
    mid = r.astype(BF16)
    lo = (r - mid.astype(F32)).astype(BF16)
    return hi, mid, lo


def _dot_x3(x, w_bf16):
    hi, mid, lo = _split3(x)
    return _dot(hi, w_bf16) + _dot(mid, w_bf16) + _dot(lo, w_bf16)


def _dot_x2(x, w_bf16):
    hi, lo = _split2(x)
    return _dot(hi, w_bf16) + _dot(lo, w_bf16)


def _block_ones(n, width):
    r = np.arange(n) // width
    return jnp.asarray((r[:, None] == r[None, :]).astype(np.float32), dtype=BF16)


def _segsum_bcast(x, ones_ref, width_block):
    cols = x.shape[1]
    outs = []
    e = ones_ref[...]
    for c in range(cols // width_block):
        outs.append(_dot_x2(x[:, c * width_block:(c + 1) * width_block], e))
    return outs[0] if len(outs) == 1 else jnp.concatenate(outs, axis=1)


def _silu(x):
    return x * jax.nn.sigmoid(x)


def _softplus(x):
    return jnp.maximum(x, 0.0) + jnp.log1p(jnp.exp(-jnp.abs(x)))


def _ada_kernel(c_ref, w_ref, b_ref, o_ref):
    c = c_ref[...]
    a = _silu(c).astype(BF16)
    o_ref[...] = _dot(a, w_ref[...].astype(BF16)) + b_ref[...]


def _ada(c_all, w_ada, b_ada):
    rows = c_all.shape[0]
    tn = 1024
    ncol = w_ada.shape[2]
    return pl.pallas_call(
        _ada_kernel,
        grid=(DEPTH, ncol // tn),
        in_specs=[
            pl.BlockSpec((rows, D_MODEL), lambda l, j: (0, 0)),
            pl.BlockSpec((None, D_MODEL, tn), lambda l, j: (l, 0, j)),
            pl.BlockSpec((None, 1, tn), lambda l, j: (l, 0, j)),
        ],
        out_specs=pl.BlockSpec((None, rows, tn), lambda l, j: (l, 0, j)),
        out_shape=jax.ShapeDtypeStruct((DEPTH, rows, ncol), F32),
        compiler_params=_cparams(("arbitrary", "arbitrary")),
        name="ada",
    )(c_all, w_ada, b_ada.reshape(DEPTH, 1, ncol))


def _norm_rows(x, g, eps):
    return x * lax.rsqrt(jnp.mean(x * x, axis=-1, keepdims=True) + eps) * g


def _norm_kernel(x_ref, g_ref, sc_ref, sh_ref, h_ref, *maybe_ht_ref):
    h = _norm_rows(x_ref[...], g_ref[...], 1e-6) * (1.0 + sc_ref[...]) + sh_ref[...]
    h_ref[...] = h.astype(BF16)
    for ht_ref in maybe_ht_ref:
        ht_ref[...] = h.T.astype(BF16)


def _norm(x, g, mod, sc_blk, sh_blk, tm, with_transpose=False):
    n, d = x.shape
    nseq, r, _ = mod.shape
    tiles_per_seq = (n // tm) // nseq
    out_specs = [pl.BlockSpec((tm, d), lambda i: (i, 0))]
    out_shape = [jax.ShapeDtypeStruct((n, d), BF16)]
    if with_transpose:
        out_specs.append(pl.BlockSpec((d, tm), lambda i: (0, i)))
        out_shape.append(jax.ShapeDtypeStruct((d, n), BF16))
    return pl.pallas_call(
        _norm_kernel,
        grid=(n // tm,),
        in_specs=[
            pl.BlockSpec((tm, d), lambda i: (i, 0)),
            pl.BlockSpec((1, d), lambda i: (0, 0)),
            pl.BlockSpec((None, r, d), lambda i: (i // tiles_per_seq, 0, sc_blk)),
            pl.BlockSpec((None, r, d), lambda i: (i // tiles_per_seq, 0, sh_blk)),
        ],
        out_specs=out_specs,
        out_shape=out_shape,
        compiler_params=_cparams(("arbitrary",)),
        name="norm",
    )(x, g.reshape(1, d), mod, mod)


def _mm_kernel(a_ref, w_ref, o_ref):
    o_ref[...] = _dot(a_ref[...], w_ref[...]).astype(o_ref.dtype)


def _mm(a, w, l, tm, tn, out_dtype=F32):
    n, d = a.shape
    ncol = w.shape[2]
    return pl.pallas_call(
        _mm_kernel,
        grid=(n // tm, ncol // tn),
        in_specs=[
            pl.BlockSpec((tm, d), lambda i, j: (i, 0)),
            pl.BlockSpec((None, d, tn), lambda i, j: (l, 0, j)),
        ],
        out_specs=pl.BlockSpec((tm, tn), lambda i, j: (i, j)),
        out_shape=jax.ShapeDtypeStruct((n, ncol), out_dtype),
        compiler_params=_cparams(("arbitrary", "arbitrary")),
        name="mm",
    )(a, w)


def _final_norm_kernel(x_ref, g_ref, o_ref):
    o_ref[...] = _norm_rows(x_ref[...], g_ref[...], 1e-6)


def _final_norm(x, g, tm):
    n, d = x.shape
    return pl.pallas_call(
        _final_norm_kernel,
        grid=(n // tm,),
        in_specs=[pl.BlockSpec((tm, d), lambda i: (i, 0)), pl.BlockSpec((1, d), lambda i: (0, 0))],
        out_specs=pl.BlockSpec((tm, d), lambda i: (i, 0)),
        out_shape=jax.ShapeDtypeStruct((n, d), F32),
        compiler_params=_cparams(("arbitrary",)),
        name="final_norm",
    )(x, g.reshape(1, d))


def _merge_kernel(o0_ref, o1_ref, o2_ref, g0_ref, g1_ref, g2_ref, w_ref, out_ref):
    acc = jax.nn.sigmoid(g0_ref[...].astype(F32)) * _dot(o0_ref[...], w_ref[0])
    acc += jax.nn.sigmoid(g1_ref[...].astype(F32)) * _dot(o1_ref[...], w_ref[1])
    acc += jax.nn.sigmoid(g2_ref[...].astype(F32)) * _dot(o2_ref[...], w_ref[2])
    out_ref[...] = acc.astype(out_ref.dtype)


def _merge(o_att, o_rwkv, o_ssm, gates, wb, l, tm, tn):
    n = o_att.shape[0]
    gblk = D_MODEL // tn
    o_spec = pl.BlockSpec((tm, BRANCH_WIDTH), lambda i, j: (i, 0))
    g_specs = [pl.BlockSpec((tm, tn), functools.partial(lambda i, j, b: (i, b * gblk + j), b=b)) for b in range(3)]
    return pl.pallas_call(
        _merge_kernel,
        grid=(n // tm, D_MODEL // tn),
        in_specs=[o_spec, o_spec, o_spec, *g_specs,
                  pl.BlockSpec((None, N_BRANCHES, BRANCH_WIDTH, tn), lambda i, j: (l, 0, 0, j))],
        out_specs=pl.BlockSpec((tm, tn), lambda i, j: (i, j)),
        out_shape=jax.ShapeDtypeStruct((n, D_MODEL), BF16),
        compiler_params=_cparams(("arbitrary", "arbitrary")),
        name="merge",
    )(o_att, o_rwkv, o_ssm, gates, gates, gates, wb)


def _proj_res_kernel(a_ref, w_ref, x_ref, g_ref, o_ref):
    o_ref[...] = x_ref[...] + g_ref[...] * _dot(a_ref[...], w_ref[...])


def _proj_res(a, w, l, x, mod, g_blk, tm, tn):
    n, k = a.shape
    nseq, r, _ = mod.shape
    tiles_per_seq = (n // tm) // nseq
    nblk = D_MODEL // tn
    return pl.pallas_call(
        _proj_res_kernel,
        grid=(n // tm, D_MODEL // tn),
        in_specs=[
            pl.BlockSpec((tm, k), lambda i, j: (i, 0)),
            pl.BlockSpec((None, k, tn), lambda i, j: (l, 0, j)),
            pl.BlockSpec((tm, tn), lambda i, j: (i, j)),
            pl.BlockSpec((None, r, tn), lambda i, j: (i // tiles_per_seq, 0, g_blk * nblk + j)),
        ],
        out_specs=pl.BlockSpec((tm, tn), lambda i, j: (i, j)),
        out_shape=jax.ShapeDtypeStruct((n, D_MODEL), F32),
        compiler_params=_cparams(("arbitrary", "arbitrary")),
        name="proj_res",
    )(a, w, x, mod)


def _rel_bucket(dist):
    n = jnp.maximum(dist, 0)
    max_exact = REL_BUCKETS // 2
    nf = jnp.maximum(n, 1).astype(F32)
    large = max_exact + (jnp.log(nf / max_exact) / math.log(REL_MAX_DIST / max_exact)
                         * (REL_BUCKETS - max_exact)).astype(jnp.int32)
    large = jnp.minimum(large, REL_BUCKETS - 1)
    return jnp.where(n < max_exact, n, large)


def _bias_lookup(rel_bias, dist):
    onehot = (_rel_bucket(dist)[..., None] == jnp.arange(REL_BUCKETS, dtype=jnp.int32)).astype(F32)
    return jnp.einsum("...b,bh->...h", onehot, rel_bias.astype(F32), precision=lax.Precision.HIGHEST)


def _bias_of_dist(rel_bias, dist):
    b = jnp.moveaxis(_bias_lookup(rel_bias, dist), -1, 0)
    return jnp.where(dist[None] >= 0, b, NEG_BIG)


def _softmax_step(s, v_bf16, m_scr, l_scr, acc_scr):
    width = min(LANES, s.shape[1])
    cols = [s[:, c:c + width] for c in range(0, s.shape[1], width)]
    m_prev = m_scr[...]
    m_new = jnp.maximum(m_prev, jnp.max(functools.reduce(jnp.maximum, cols), axis=-1, keepdims=True))
    alpha = jnp.exp(m_prev - m_new)
    ps = [jnp.exp(c - m_new[:, :width]) for c in cols]
    l_scr[...] = alpha * l_scr[...] + jnp.sum(functools.reduce(jnp.add, ps), axis=-1, keepdims=True)
    p = ps[0] if len(ps) == 1 else jnp.concatenate(ps, axis=1)
    acc_scr[...] = alpha * acc_scr[...] + _dot(p.astype(BF16), v_bf16)
    m_scr[...] = m_new


def _two_map_queries(q, scale):
    lane = lax.broadcasted_iota(jnp.int32, q.shape, 1)
    qs = q * scale
    q0 = jnp.where(lane < ATT_DH, qs, 0.0)
    q1 = jnp.where(lane >= ATT_DH, qs, 0.0)
    return jnp.concatenate([q0, q1], axis=0).astype(BF16)


def _diff_finalize(acc, l, lam, g, rows, post_scale):
    o0 = acc[:rows] / l[:rows]
    o1 = acc[rows:] / l[rows:]
    o = o0 - lam * o1
    o = o * lax.rsqrt(jnp.mean(o * o, axis=-1, keepdims=True) + 1e-5) * g
    return o * post_scale


def _attn_prompt_kernel(qi_ref, ki_ref, lam_ref, q_ref, k_ref, v_ref, bias_ref, g_ref, o_ref,
                        m_scr, l_scr, acc_scr, *, tq, hps, rb, post_scale):
    p = pl.program_id(2)
    qi = qi_ref[p]
    ki = ki_ref[p]

    @pl.when(ki == 0)
    def _():
        m_scr[...] = jnp.full(m_scr.shape, NEG_BIG, F32)
        l_scr[...] = jnp.zeros(l_scr.shape, F32)
        acc_scr[...] = jnp.zeros(acc_scr.shape, F32)

    for hh in range(hps):
        cols = slice(hh * LANES, (hh + 1) * LANES)
        kb = k_ref[:, cols].astype(BF16)
        vb = v_ref[:, cols].astype(BF16)
        for r0 in range(0, tq, rb):
            qs = q_ref[r0:r0 + rb, cols] * (ATT_DH ** -0.5)
            lane = lax.broadcasted_iota(jnp.int32, qs.shape, 1)
            b = bias_ref[hh, r0:r0 + rb, :]
            for mp in range(2):
                qm = jnp.where((lane < ATT_DH) if mp == 0 else (lane >= ATT_DH), qs, 0.0).astype(BF16)
                rows = pl.ds(mp * tq + r0, rb)
                _softmax_step(_dot_nt(qm, kb) + b, vb, m_scr.at[hh, rows], l_scr.at[hh, rows], acc_scr.at[hh, rows])

    @pl.when(ki == qi)
    def _():
        for hh in range(hps):
            o = _diff_finalize(acc_scr[hh], l_scr[hh], lam_ref[0, 0], g_ref[...], tq, post_scale)
            o_ref[:, hh * LANES:(hh + 1) * LANES] = o.astype(o_ref.dtype)


def _attn_prompt(qkv, rel_bias, lam, subln, lam_init, bsz, t, tq, hps=4, rb=None):
    rb = tq if rb is None else rb
    nq = t // tq
    pairs = [(a, b) for a in range(nq) for b in range(a + 1)]
    qi_tab = jnp.asarray([a for a, _ in pairs], jnp.int32)
    ki_tab = jnp.asarray([b for _, b in pairs], jnp.int32)
    assert tq % REL_MAX_DIST == 0
    nblk = tq // REL_MAX_DIST
    ii = jnp.arange(REL_MAX_DIST, dtype=jnp.int32)
    d0 = ii[:, None] - ii[None, :]
    blocks = [_bias_of_dist(rel_bias, d0 + k * REL_MAX_DIST) for k in range(3)]
    masked = jnp.full_like(blocks[0], NEG_BIG)

    def tile(tile_dist):
        rows = []
        for bi in range(nblk):
            row = []
            for bj in range(nblk):
                dblk = tile_dist * nblk + bi - bj
                row.append(masked if dblk < 0 else blocks[min(dblk, 2)])
            rows.append(jnp.concatenate(row, axis=2))
        return jnp.concatenate(rows, axis=1)

    tiles = jnp.stack([tile(0), tile(1), tile(2)], axis=1)
    qkv3 = qkv.reshape(bsz, t, 3 * ATT_QK)
    kern = functools.partial(_attn_prompt_kernel, tq=tq, hps=hps, rb=rb, post_scale=1.0 - lam_init)
    assert ATT_HEADS % hps == 0
    hgroups = ATT_HEADS // hps
    wide = hps * LANES
    grid_spec = pltpu.PrefetchScalarGridSpec(
        num_scalar_prefetch=2,
        grid=(bsz, hgroups, len(pairs)),
        in_specs=[
            pl.BlockSpec(memory_space=pltpu.SMEM),
            pl.BlockSpec((None, tq, wide), lambda b, h, p, qt, kt: (b, qt[p], h)),
            pl.BlockSpec((None, tq, wide), lambda b, h, p, qt, kt: (b, kt[p], hgroups + h)),
            pl.BlockSpec((None, tq, wide), lambda b, h, p, qt, kt: (b, kt[p], 2 * hgroups + h)),
            pl.BlockSpec((hps, None, tq, tq), lambda b, h, p, qt, kt: (h, jnp.minimum(qt[p] - kt[p], 2), 0, 0)),
            pl.BlockSpec((1, LANES), lambda b, h, p, qt, kt: (0, 0)),
        ],
        out_specs=pl.BlockSpec((None, tq, wide), lambda b, h, p, qt, kt: (b, qt[p], h)),
        scratch_shapes=[pltpu.VMEM((hps, 2 * tq, LANES), F32), pltpu.VMEM((hps, 2 * tq, LANES), F32),
                        pltpu.VMEM((hps, 2 * tq, LANES), F32)],
    )
    out = pl.pallas_call(
        kern,
        grid_spec=grid_spec,
        out_shape=jax.ShapeDtypeStruct((bsz, t, BRANCH_WIDTH), BF16),
        compiler_params=_cparams(("arbitrary", "arbitrary", "arbitrary")),
        name="attn_prompt",
    )(qi_tab, ki_tab, lam.reshape(1, 1), qkv3, qkv3, qkv3, tiles, subln.reshape(1, ATT_VD))
    return out.reshape(bsz * t, BRANCH_WIDTH)


def _attn_sample_kernel(pt_ref, lam_ref, q_ref, kn_ref, vn_ref, bias_ref, biasn_ref, g_ref, *rest,
                        pps, rows, post_scale):
    kp_refs = rest[:pps]
    vp_refs = rest[pps:2 * pps]
    o_ref, m_scr, l_scr, acc_scr = rest[2 * pps:]
    s_idx = pl.program_id(1)
    last = s_idx == pl.num_programs(1) - 1

    @pl.when(s_idx == 0)
    def _():
        m_scr[...] = jnp.full(m_scr.shape, NEG_BIG, F32)
        l_scr[...] = jnp.zeros(l_scr.shape, F32)
        acc_scr[...] = jnp.zeros(acc_scr.shape, F32)

    qq = _two_map_queries(q_ref[...], ATT_DH ** -0.5)
    scores = []
    values = []
    for r in range(pps):
        kb = kp_refs[r][...].reshape(PAGE_SIZE * ATT_HEADS, LANES).astype(BF16)
        values.append(vp_refs[r][...].reshape(PAGE_SIZE * ATT_HEADS, LANES).astype(BF16))
        s = _dot_nt(qq, kb)
        if r == pps - 1:
            b = jnp.where(last, bias_ref[1], bias_ref[0])
        else:
            b = bias_ref[0]
        scores.append(s + jnp.concatenate([b, b], axis=0))
    _softmax_step(jnp.concatenate(scores, axis=1), jnp.concatenate(values, axis=0), m_scr, l_scr, acc_scr)

    @pl.when(last)
    def _():
        s = _dot_nt(qq, kn_ref[...].astype(BF16))
        b = biasn_ref[...]
        s = s + jnp.concatenate([b, b], axis=0)
        _softmax_step(s, vn_ref[...].astype(BF16), m_scr, l_scr, acc_scr)
        o = _diff_finalize(acc_scr[...], l_scr[...], lam_ref[0, 0], g_ref[...], rows, post_scale)
        o_ref[...] = o.astype(o_ref.dtype)


def _attn_sample(qkv, cache_k, cache_v, page_table, layer, rel_bias, lam, subln, lam_init, bsz, t):
    n_pages = page_table.shape[1]
    past = n_pages * PAGE_SIZE
    pps = 8
    assert n_pages % pps == 0
    rows = t * ATT_HEADS
    q = qkv[:, :ATT_QK].reshape(bsz, rows, LANES)
    kn = qkv[:, ATT_QK:2 * ATT_QK].reshape(bsz, rows, LANES)
    vn = qkv[:, 2 * ATT_QK:].reshape(bsz, rows, LANES)
    tok = jnp.repeat(jnp.arange(t, dtype=jnp.int32), ATT_HEADS)
    hq = jnp.tile(jnp.arange(ATT_HEADS, dtype=jnp.int32), t)
    kk = jnp.repeat(jnp.arange(PAGE_SIZE, dtype=jnp.int32), ATT_HEADS)
    hk = jnp.tile(jnp.arange(ATT_HEADS, dtype=jnp.int32), PAGE_SIZE)

    head_onehot = (hq[:, None] == jnp.arange(ATT_HEADS, dtype=jnp.int32)).astype(F32)

    def tile(dist, hk_):
        b = jnp.sum(_bias_lookup(rel_bias, dist) * head_onehot[:, None, :], axis=-1)
        return jnp.where((dist >= 0) & (hq[:, None] == hk_[None, :]), b, NEG_BIG)

    assert PAGE_SIZE >= REL_MAX_DIST
    far = tile(jnp.broadcast_to(jnp.int32(2 * PAGE_SIZE), (rows, PAGE_SIZE * ATT_HEADS)) + tok[:, None] - kk[None, :], hk)
    near = tile(PAGE_SIZE + tok[:, None] - kk[None, :], hk)
    bias_pages = jnp.stack([far, near])
    bias_new = tile(tok[:, None] - tok[None, :], hq)
    kern = functools.partial(_attn_sample_kernel, pps=pps, rows=rows, post_scale=1.0 - lam_init)
    page_block = (None, None, PAGE_SIZE, ATT_HEADS, LANES)
    page_specs = [pl.BlockSpec(page_block, functools.partial(lambda b, s, pt, r: (pt[b, s * pps + r], layer, 0, 0, 0), r=r))
                  for r in range(pps)]
    grid_spec = pltpu.PrefetchScalarGridSpec(
        num_scalar_prefetch=1,
        grid=(bsz, n_pages // pps),
        in_specs=[
            pl.BlockSpec(memory_space=pltpu.SMEM),
            pl.BlockSpec((None, rows, LANES), lambda b, s, pt: (b, 0, 0)),
            pl.BlockSpec((None, rows, LANES), lambda b, s, pt: (b, 0, 0)),
            pl.BlockSpec((None, rows, LANES), lambda b, s, pt: (b, 0, 0)),
            pl.BlockSpec((2, rows, PAGE_SIZE * ATT_HEADS), lambda b, s, pt: (0, 0, 0)),
            pl.BlockSpec((rows, rows), lambda b, s, pt: (0, 0)),
            pl.BlockSpec((1, LANES), lambda b, s, pt: (0, 0)),
            *page_specs, *page_specs,
        ],
        out_specs=pl.BlockSpec((None, rows, LANES), lambda b, s, pt: (b, 0, 0)),
        scratch_shapes=[pltpu.VMEM((2 * rows, LANES), F32), pltpu.VMEM((2 * rows, LANES), F32),
                        pltpu.VMEM((2 * rows, LANES), F32)],
    )
    out = pl.pallas_call(
        kern,
        grid_spec=grid_spec,
        out_shape=jax.ShapeDtypeStruct((bsz, rows, LANES), BF16),
        compiler_params=_cparams(("arbitrary", "arbitrary")),
        name="attn_sample",
    )(page_table, lam.reshape(1, 1), q, kn, vn, bias_pages, bias_new, subln.reshape(1, ATT_VD),
      *([cache_k] * pps), *([cache_v] * pps))
    return out.reshape(bsz * t, BRANCH_WIDTH)


RW_LORA_OFF = 3 * BRANCH_WIDTH
RW_G_OFF = RW_LORA_OFF + LANES


def _rwkv_prep_kernel(x_ref, prev_ref, shift_ref, mu_ref, w0_ref, w2_ref, a0_ref, a2_ref, g2_ref,
                      kkw_ref, ka_ref, rk_ref, e_ref,
                      a_out, q_out, w_out, b_out, k_out, vkr_out, v_out, bon_out, g_out):
    bw = BRANCH_WIDTH
    x = x_ref[...]
    tm = x.shape[0]
    first = pl.program_id(1) == 0
    prev_row = jnp.where(first, shift_ref[...], prev_ref[SUBLANES - 1:SUBLANES, :])
    row = lax.broadcasted_iota(jnp.int32, x.shape, 0)
    prev = jnp.where(row == 0, prev_row, pltpu.roll(x, 1, axis=0))
    pm = x + (prev - x) * mu_ref[...]
    r = pm[:, :bw]
    k = pm[:, bw:2 * bw]
    v = pm[:, 2 * bw:3 * bw]
    lora = pm[:, RW_LORA_OFF:RW_LORA_OFF + LANES]
    glo = pm[:, RW_G_OFF:RW_G_OFF + 2 * LANES]
    w = -_softplus(-(w0_ref[...] + _dot(jnp.tanh(lora).astype(BF16), w2_ref[...]))) - 0.5
    decay = jnp.exp(-jnp.exp(w))
    a = jax.nn.sigmoid(a0_ref[...] + _dot(lora.astype(BF16), a2_ref[...]))
    g = _dot(jax.nn.sigmoid(glo).astype(BF16), g2_ref[...])
    kk = k * kkw_ref[...]
    n2 = _segsum_bcast(kk * kk, e_ref, LANES)
    kk = kk / jnp.maximum(jnp.sqrt(n2), 1e-12)
    k2 = k * (1.0 + (a - 1.0) * ka_ref[...])
    bm = kk * a
    a_out[...] = -kk
    q_out[...] = decay * r - kk * _segsum_bcast(bm * r, e_ref, LANES)
    w_out[...] = decay
    b_out[...] = bm
    k_out[...] = k2
    vkr_out[...] = v * _segsum_bcast(k2 * r, e_ref, LANES)
    v_out[...] = v
    bon_out[...] = _segsum_bcast(r * k2 * rk_ref[...], e_ref, LANES) * v
    g_out[...] = g


def _rwkv_prep(rw, shift0, P, bsz, t, tm):
    n = rw.shape[0]
    bw = BRANCH_WIDTH
    cp = RWKV_COLS_PAD
    tiles = t // tm
    vec = lambda nm: pl.BlockSpec((1, bw), lambda b, i: (0, 0))
    out_spec = pl.BlockSpec((tm, bw), lambda b, i: (b * tiles + i, 0))
    rows8 = tm // SUBLANES
    outs = pl.pallas_call(
        _rwkv_prep_kernel,
        grid=(bsz, tiles),
        in_specs=[
            pl.BlockSpec((tm, cp), lambda b, i: (b * tiles + i, 0)),
            pl.BlockSpec((SUBLANES, cp), lambda b, i: (jnp.maximum((b * tiles + i) * rows8 - 1, 0), 0)),
            pl.BlockSpec((None, 1, cp), lambda b, i: (b, 0, 0)),
            pl.BlockSpec((1, cp), lambda b, i: (0, 0)),
            vec("w0"),
            pl.BlockSpec((LANES, bw), lambda b, i: (0, 0)),
            vec("a0"),
            pl.BlockSpec((LANES, bw), lambda b, i: (0, 0)),
            pl.BlockSpec((2 * LANES, bw), lambda b, i: (0, 0)),
            vec("kk"), vec("ka"), vec("rk"),
            pl.BlockSpec((LANES, LANES), lambda b, i: (0, 0)),
        ],
        out_specs=[out_spec] * 9,
        out_shape=[jax.ShapeDtypeStruct((n, bw), F32)] * 9,
        compiler_params=_cparams(("arbitrary", "arbitrary")),
        name="rwkv_prep",
    )(rw, rw, shift0.reshape(bsz, 1, cp), P["mu"], P["w0"], P["w2"], P["a0"], P["a2"], P["g2"],
      P["kk"], P["ka"], P["rk"], _block_ones(LANES, RWKV_DH))
    return outs


def _rwkv_scan_kernel(a_ref, q_ref, w_ref, b_ref, k_ref, vkr_ref, v_ref, bon_ref, g_ref,
                      s0_ref, lng_ref, lnb_ref, e_ref, o_ref, sT_ref, s_scr, y_scr, *, bb, tc, pg, cpd):
    c = pl.program_id(2)

    @pl.when(c == 0)
    def _():
        s_scr[...] = s0_ref[...]

    e = e_ref[...]
    sub = lax.broadcasted_iota(jnp.int32, (RWKV_DH, LANES), 0)
    lane = lax.broadcasted_iota(jnp.int32, (RWKV_DH, LANES), 1)
    diag = (lane % RWKV_DH) == sub

    chains = [(b, p) for b in range(bb) for p in range(pg)]

    def group(tg, carry):
        t0 = pl.multiple_of(tg * SUBLANES, SUBLANES)
        states = [s_scr[b, p] for b, p in chains]
        ys = [[] for _ in chains]
        for r in range(SUBLANES):
            for c0 in range(0, len(chains), cpd):
                group_chains = list(enumerate(chains))[c0:c0 + cpd]
                row = lambda ref, b, p: ref[b, pl.ds(t0, SUBLANES), pl.ds(p * LANES, LANES)][r:r + 1, :]
                parts = []
                for ci, (b, p) in group_chains:
                    s = states[ci]
                    parts += [(s * row(a_ref, b, p)).astype(BF16), (s * row(q_ref, b, p)).astype(BF16),
                              jnp.where(diag, row(v_ref, b, p), 0.0).astype(BF16)]
                res = _dot(jnp.concatenate(parts, axis=0), e)
                for gi, (ci, (b, p)) in enumerate(group_chains):
                    base = 3 * RWKV_DH * gi
                    sa = res[base:base + RWKV_DH]
                    m2 = res[base + RWKV_DH:base + 2 * RWKV_DH]
                    vb = res[base + 2 * RWKV_DH:base + 3 * RWKV_DH]
                    states[ci] = states[ci] * row(w_ref, b, p) + sa * row(b_ref, b, p) + vb * row(k_ref, b, p)
                    ys[ci].append(jnp.sum(jnp.where(diag, m2, 0.0), axis=0, keepdims=True) + row(vkr_ref, b, p))
        for ci, (b, p) in enumerate(chains):
            s_scr[b, p] = states[ci]
            y_scr[b, pl.ds(t0, SUBLANES), pl.ds(p * LANES, LANES)] = jnp.concatenate(ys[ci], axis=0)
        return carry

    lax.fori_loop(0, tc // SUBLANES, group, 0)

    for b in range(bb):
        y = y_scr[b]
        mu = _segsum_bcast(y, e_ref, LANES) * (1.0 / RWKV_DH)
        d = y - mu
        var = _segsum_bcast(d * d, e_ref, LANES) * (1.0 / RWKV_DH)
        yn = d * lax.rsqrt(var + RWKV_LN_EPS) * lng_ref[...] + lnb_ref[...]
        o_ref[b] = ((yn + bon_ref[b]) * g_ref[b]).astype(o_ref.dtype)

    @pl.when(c == pl.num_programs(2) - 1)
    def _():
        sT_ref[...] = s_scr[...]


def _rwkv_scan(prep, s0, lng, lnb, bsz, t, bb, tc, pg, cpd=2):
    bw = BRANCH_WIDTH
    npair = RWKV_HEADS // 2
    assert tc % SUBLANES == 0 and npair % pg == 0
    wcols = pg * LANES
    seq = [z.reshape(bsz, t, bw) for z in prep]
    blk = pl.BlockSpec((bb, tc, wcols), lambda b, q, c: (b, c, q))
    st = pl.BlockSpec((bb, pg, RWKV_DH, LANES), lambda b, q, c: (b, q, 0, 0))
    vec = pl.BlockSpec((1, wcols), lambda b, q, c: (0, q))
    assert (bb * pg) % cpd == 0
    kern = functools.partial(_rwkv_scan_kernel, bb=bb, tc=tc, pg=pg, cpd=cpd)
    o, s_t = pl.pallas_call(
        kern,
        grid=(bsz // bb, npair // pg, t // tc),
        in_specs=[blk] * 9 + [st, vec, vec, pl.BlockSpec((LANES, LANES), lambda b, q, c: (0, 0))],
        out_specs=[blk, st],
        out_shape=[jax.ShapeDtypeStruct((bsz, t, bw), BF16),
                   jax.ShapeDtypeStruct((bsz, npair, RWKV_DH, LANES), F32)],
        scratch_shapes=[pltpu.VMEM((bb, pg, RWKV_DH, LANES), F32), pltpu.VMEM((bb, tc, wcols), F32)],
        compiler_params=_cparams(("arbitrary", "arbitrary", "arbitrary")),
        name="rwkv_scan",
    )(*seq, s0, lng, lnb, _block_ones(LANES, RWKV_DH))
    return o.reshape(bsz * t, bw), s_t


def _mamba_kernel(x_ref, conv0_ref, h0_ref, cw_ref, cb_ref, dtb_ref, alog_ref, dvec_ref, nrm_ref,
                  tri_ref, eye_ref, xp_ref, xpt_ref, e256_ref, o_ref, hT_ref, tail_scr, h_scr, *, ln):
    bw = BRANCH_WIDTH
    gn = SSM_GROUPS * SSM_STATE
    npair = SSM_HEADS // 2
    c = pl.program_id(1)

    @pl.when(c == 0)
    def _():
        tail_scr[...] = conv0_ref[...]
        h_scr[...] = h0_ref[...]

    blk = x_ref[...]
    z = blk[:, :bw]
    xbc = blk[:, bw:bw + SSM_CONV_DIM]
    dtr = blk[:, bw + SSM_CONV_DIM:bw + SSM_CONV_DIM + LANES]
    xpad = jnp.concatenate([tail_scr[...], xbc], axis=0)
    conv = cb_ref[...]
    for i in range(SSM_CONV):
        off = SUBLANES - (SSM_CONV - 1) + i
        conv = conv + xpad[off:off + ln] * cw_ref[i:i + 1, :]
    tail_scr[...] = xpad[ln:ln + SUBLANES]
    conv = _silu(conv)
    xs = conv[:, :bw]
    lane = lax.broadcasted_iota(jnp.int32, (ln, LANES), 1)
    dt = jnp.where(lane < SSM_HEADS, _softplus(dtr + dtb_ref[...]), 0.0)
    a_neg = -jnp.exp(alog_ref[...])
    a = dt * a_neg
    ah, am_, al = _split3(a)
    tri = tri_ref[...]
    a_cs = _dot(tri, ah) + _dot(tri, am_) + _dot(tri, al)
    ch, cm, cl = _split3(a_cs)
    eye = eye_ref[...]
    a_cs_t = _dot_nt(eye, ch) + _dot_nt(eye, cm) + _dot_nt(eye, cl)
    a_tot = a_cs[ln - 1:ln, :]
    xp = xp_ref[...]
    dt_x = _dot_x3(dt, xp)
    acs_x = _dot_x3(a_cs, xp)
    atot_x = _dot_x3(jnp.broadcast_to(a_tot, (SUBLANES, LANES)), xp)[0:1]
    th, tm_, tl = _split3(jnp.broadcast_to(a_tot, (SUBLANES, LANES)))
    xpt = xpt_ref[...]
    atot_col = (_dot_nt(xpt, th) + _dot_nt(xpt, tm_) + _dot_nt(xpt, tl))[:, 0:1]
    xc = xs * dt_x
    xd = xc * jnp.exp(atot_x - acs_x)
    row_i = lax.broadcasted_iota(jnp.int32, (ln, ln), 0)
    col_i = lax.broadcasted_iota(jnp.int32, (ln, ln), 1)
    tril = row_i >= col_i
    lane_h = lax.broadcasted_iota(jnp.int32, (ln, LANES), 1) < SSM_HEAD_DIM
    y_parts = []
    for g in range(SSM_GROUPS):
        bg = conv[:, bw + g * SSM_STATE:bw + (g + 1) * SSM_STATE].astype(BF16)
        cg = conv[:, bw + gn + g * SSM_STATE:bw + gn + (g + 1) * SSM_STATE].astype(BF16)
        cbm = _dot_nt(cg, bg)
        for pr in range(2):
            pi = g * 2 + pr
            cs = slice(pi * LANES, (pi + 1) * LANES)
            xcp = xc[:, cs]
            y = None
            for e in range(2):
                hd = 2 * pi + e
                seg = a_cs[:, hd:hd + 1] - a_cs_t[hd:hd + 1, :]
                wm = jnp.where(tril, cbm * jnp.exp(jnp.where(tril, seg, 0.0)), 0.0).astype(BF16)
                xm = jnp.where(lane_h if e == 0 else jnp.logical_not(lane_h), xcp, 0.0).astype(BF16)
                part = _dot(wm, xm)
                y = part if y is None else y + part
            hp = h_scr[pi]
            y = y + _dot_nt(cg, hp.astype(BF16)) * jnp.exp(acs_x[:, cs])
            y_parts.append(y)
            h_new = jnp.exp(atot_col[pi * LANES:(pi + 1) * LANES, :]) * hp + _dot_tn(xd[:, cs].astype(BF16), bg)
            h_scr[pi] = h_new
    y = jnp.concatenate(y_parts, axis=1)
    y = (y + dvec_ref[...] * xs) * _silu(z)
    gw = bw // SSM_GROUPS
    ms = _segsum_bcast(y * y, e256_ref, gw) * (1.0 / gw)
    y = y * lax.rsqrt(ms + 1e-5) * nrm_ref[...]
    o_ref[...] = y.astype(o_ref.dtype)

    @pl.when(c == pl.num_programs(1) - 1)
    def _():
        hT_ref[...] = h_scr[...]


def _mamba(sm, conv0, h0, P, bsz, t):
    bw = BRANCH_WIDTH
    ln = min(SSM_CHUNK, t)
    assert t % ln == 0
    nc = t // ln
    npair = SSM_HEADS // 2
    tri = jnp.asarray(np.tril(np.ones((ln, ln), np.float32)), dtype=BF16)
    eye = jnp.asarray(np.eye(LANES, dtype=np.float32), dtype=BF16)
    xp_np = (np.arange(LANES)[:, None] == (np.arange(bw)[None, :] // SSM_HEAD_DIM)).astype(np.float32)
    xp = jnp.asarray(xp_np, dtype=BF16)
    xpt = jnp.asarray(xp_np.T.copy(), dtype=BF16)
    const = lambda shape: pl.BlockSpec(shape, lambda b, c: tuple(0 for _ in shape))
    kern = functools.partial(_mamba_kernel, ln=ln)
    o, h_t = pl.pallas_call(
        kern,
        grid=(bsz, nc),
        in_specs=[
            pl.BlockSpec((ln, SSM_COLS_PAD), lambda b, c: (b * nc + c, 0)),
            pl.BlockSpec((None, SUBLANES, SSM_CONV_DIM), lambda b, c: (b, 0, 0)),
            pl.BlockSpec((None, npair, LANES, SSM_STATE), lambda b, c: (b, 0, 0, 0)),
            const((SSM_CONV, SSM_CONV_DIM)), const((1, SSM_CONV_DIM)), const((1, LANES)), const((1, LANES)),
            const((1, bw)), const((1, bw)), const((ln, ln)), const((LANES, LANES)),
            const((LANES, bw)), const((bw, LANES)), const((bw // SSM_GROUPS, bw // SSM_GROUPS)),
        ],
        out_specs=[pl.BlockSpec((ln, bw), lambda b, c: (b * nc + c, 0)),
                   pl.BlockSpec((None, npair, LANES, SSM_STATE), lambda b, c: (b, 0, 0, 0))],
        out_shape=[jax.ShapeDtypeStruct((bsz * t, bw), BF16),
                   jax.ShapeDtypeStruct((bsz, npair, LANES, SSM_STATE), F32)],
        scratch_shapes=[pltpu.VMEM((SUBLANES, SSM_CONV_DIM), F32), pltpu.VMEM((npair, LANES, SSM_STATE), F32)],
        compiler_params=_cparams(("arbitrary", "arbitrary")),
        name="mamba",
    )(sm, conv0, h0, P["cw"], P["cb"], P["dtb"], P["alog"], P["dvec"], P["nrm"], tri, eye, xp, xpt,
      _block_ones(bw // SSM_GROUPS, bw // SSM_GROUPS))
    return o, h_t


PEER_NO_RANK = float(PEER_NKEYS)


def _topk_rows(s, k, with_rank=False):
    rows = s.shape[0]
    iota = lax.broadcasted_iota(jnp.int32, s.shape, 0)
    rank = jnp.full(s.shape, PEER_NO_RANK, F32)
    outs = []
    for i in range(k):
        m = jnp.max(s, axis=0, keepdims=True)
        outs.append(m)
        idx = jnp.min(jnp.where(s == m, iota, rows), axis=0, keepdims=True)
        hit = iota == idx
        s = jnp.where(hit, -jnp.inf, s)
        if with_rank:
            rank = jnp.where(hit, float(i), rank)
    top = jnp.concatenate(outs, axis=0)
    return (top, rank) if with_rank else top


def _peer_route_kernel(q_ref, k1_ref, k2_ref, cnt_ref, rank2_ref, e1_ref, e2_ref):
    sc = []
    for half, k_ref in enumerate((k1_ref, k2_ref)):
        kh, kl = _split2(k_ref[...])
        qh, ql = _split2(q_ref[:, half * PEER_HALF:(half + 1) * PEER_HALF])
        sc.append(_dot_nt(kh, qh) + _dot_nt(kh, ql) + _dot_nt(kl, qh))
    v1, rank1 = _topk_rows(sc[0], PEER_TOPK, with_rank=True)
    v2, rank2 = _topk_rows(sc[1], PEER_TOPK, with_rank=True)
    row8 = lax.broadcasted_iota(jnp.int32, (SUBLANES, v2.shape[1]), 0)
    pieces = [v1[0:1, :] + v2]
    for i in range(1, SUBLANES):
        n_i = PEER_TOPK // (i + 1)
        pieces.append(jnp.where(row8 < n_i, v1[i:i + 1, :] + v2[0:SUBLANES, :], -jnp.inf))
    pieces.append(v1[SUBLANES:PEER_TOPK, :] + v2[0:1, :])
    cand = jnp.concatenate(pieces, axis=0)
    top = _topk_rows(cand, PEER_TOPK)
    tau = top[PEER_TOPK - 1:PEER_TOPK, :]
    zsum = jnp.sum(jnp.exp(top - top[0:1, :]), axis=0, keepdims=True)
    cnt = jnp.zeros(rank1.shape, F32)
    for i in range(PEER_TOPK):
        c_i = jnp.sum(((v1[i:i + 1, :] + v2) >= tau).astype(F32), axis=0, keepdims=True)
        cnt = jnp.where(rank1 == float(i), c_i, cnt)
    cnt_ref[...] = cnt
    rank2_ref[...] = rank2.astype(BF16)
    e1_ref[...] = jnp.exp(sc[0] - v1[0:1, :]) / zsum
    e2_ref[...] = jnp.exp(sc[1] - v2[0:1, :]).astype(BF16)


def _peer_route(q, k1, k2, tt):
    n = q.shape[0]
    arr = lambda dt: jax.ShapeDtypeStruct((PEER_HEADS, PEER_NKEYS, n), dt)
    spec = pl.BlockSpec((None, PEER_NKEYS, tt), lambda i, h: (h, 0, i))
    return pl.pallas_call(
        _peer_route_kernel,
        grid=(n // tt, PEER_HEADS),
        in_specs=[pl.BlockSpec((tt, PEER_KEY_DIM), lambda i, h: (i, h)),
                  pl.BlockSpec((PEER_NKEYS, PEER_HALF), lambda i, h: (0, 0)),
                  pl.BlockSpec((PEER_NKEYS, PEER_HALF), lambda i, h: (0, 0))],
        out_specs=[spec, spec, spec, spec],
        out_shape=[arr(F32), arr(BF16), arr(F32), arr(BF16)],
        compiler_params=_cparams(("arbitrary", "arbitrary")),
        name="peer_route",
    )(q, k1, k2)


def _peer_dense_kernel(ht_ref, u_ref, vt_ref, cnt_ref, rank2_ref, e1_ref, e2_ref, o_ref, act_scr, ca_scr,
                       *, ib, nsplit):
    j = pl.program_id(1)

    @pl.when(j == 0)
    def _():
        o_ref[...] = jnp.zeros(o_ref.shape, F32)

    ht = ht_ref[...]
    per = ib // nsplit
    rows = per * PEER_NKEYS
    for part in range(nsplit):
        act_scr[part * rows:(part + 1) * rows, :] = _dot(u_ref[part * rows:(part + 1) * rows, :], ht)
    for part in range(nsplit):
        for ii in range(part * per, (part + 1) * per):
            sl = slice(ii * PEER_NKEYS, (ii + 1) * PEER_NKEYS)
            act = act_scr[sl, :]
            act = 0.5 * act * (1.0 + lax.erf(act * (2.0 ** -0.5)))
            coef = jnp.zeros(act.shape, BF16)
            for h in range(PEER_HEADS):
                keep = rank2_ref[h] < cnt_ref[h, ii:ii + 1, :].astype(BF16)
                coef = coef + jnp.where(keep, e2_ref[h], 0.0) * e1_ref[h, ii:ii + 1, :].astype(BF16)
            ca_scr[sl, :] = coef * act.astype(BF16)
        o_ref[...] += _dot(vt_ref[:, part * rows:(part + 1) * rows], ca_scr[part * rows:(part + 1) * rows, :])


def _peer_dense(ht, u, vt, l, cnt, rank2, e1, e2, tt, ib, nsplit=2):
    d, n = ht.shape
    eb = ib * PEER_NKEYS
    kern = functools.partial(_peer_dense_kernel, ib=ib, nsplit=nsplit)
    full = pl.BlockSpec((PEER_HEADS, PEER_NKEYS, tt), lambda i, j: (0, 0, i))
    part = pl.BlockSpec((PEER_HEADS, ib, tt), lambda i, j: (0, j, i))
    return pl.pallas_call(
        kern,
        grid=(n // tt, PEER_EXPERTS // eb),
        in_specs=[
            pl.BlockSpec((d, tt), lambda i, j: (0, i)),
            pl.BlockSpec((None, eb, d), lambda i, j: (l, j, 0)),
            pl.BlockSpec((None, d, eb), lambda i, j: (l, 0, j)),
            part, full, part, full,
        ],
        out_specs=pl.BlockSpec((d, tt), lambda i, j: (0, i)),
        out_shape=jax.ShapeDtypeStruct((d, n), F32),
        scratch_shapes=[pltpu.VMEM((eb, tt), F32), pltpu.VMEM((eb, tt), BF16)],
        compiler_params=_cparams(("arbitrary", "arbitrary")),
        name="peer_dense",
    )(ht, u, vt, cnt, rank2, e1, e2)


def _peer_res_kernel(x_ref, ot_ref, g_ref, o_ref):
    o_ref[...] = x_ref[...] + g_ref[...] * ot_ref[...].T


def _peer_res(x, out_t, mod, g_blk, tm):
    n, d = x.shape
    nseq, r, _ = mod.shape
    tiles_per_seq = (n // tm) // nseq
    return pl.pallas_call(
        _peer_res_kernel,
        grid=(n // tm,),
        in_specs=[pl.BlockSpec((tm, d), lambda i: (i, 0)),
                  pl.BlockSpec((d, tm), lambda i: (0, i)),
                  pl.BlockSpec((None, r, d), lambda i: (i // tiles_per_seq, 0, g_blk))],
        out_specs=pl.BlockSpec((tm, d), lambda i: (i, 0)),
        out_shape=jax.ShapeDtypeStruct((n, d), F32),
        compiler_params=_cparams(("arbitrary",)),
        name="peer_res",
    )(x, out_t, mod)


def _pad_cols(a, width):
    return jnp.pad(a, [(0, 0)] * (a.ndim - 1) + [(0, width - a.shape[-1])])


def _pad_rows(a, rows):
    return jnp.pad(a, [(0, rows - a.shape[0])] + [(0, 0)] * (a.ndim - 1))


def _big_weights(W):
    w_in = W["w_in"]
    return {
        "w_qkv": w_in[:, :, :RWKV_OFF].astype(BF16),
        "w_rw": _pad_cols(w_in[:, :, RWKV_OFF:SSM_OFF], RWKV_COLS_PAD).astype(BF16),
        "w_sm": _pad_cols(w_in[:, :, SSM_OFF:GATE_OFF], SSM_COLS_PAD).astype(BF16),
        "w_gate": w_in[:, :, GATE_OFF:].astype(BF16),
        "wb": W["w_branch"].astype(BF16),
        "wo": W["w_o"].astype(BF16),
        "wq": W["peer_wq"].astype(BF16),
        "u": W["peer_u"].astype(BF16),
        "vt": jnp.swapaxes(W["peer_v"].astype(BF16), 1, 2),
    }


def _layer_weights(l, W, big):
    bw = BRANCH_WIDTH
    out = dict(big)
    o1 = 3 * bw
    o2 = o1 + RWKV_DECAY_LORA
    o3 = o2 + RWKV_AAA_LORA
    rwp = {
        "mu": _pad_cols(W["rwkv_mu"][l][None, :], RWKV_COLS_PAD),
        "w0": W["rwkv_w0"][l][None, :],
        "w2": _pad_rows(W["rwkv_w2"][l], LANES).astype(BF16),
        "a0": W["rwkv_a0"][l][None, :],
        "a2": jnp.concatenate([jnp.zeros((RWKV_DECAY_LORA, bw), F32), W["rwkv_a2"][l]], axis=0).astype(BF16),
        "g2": _pad_rows(W["rwkv_g2"][l], 2 * LANES).astype(BF16),
        "kk": W["rwkv_kk"][l][None, :],
        "ka": W["rwkv_ka"][l][None, :],
        "rk": W["rwkv_rk"][l].reshape(1, bw),
        "lng": W["rwkv_lnx_g"][l][None, :],
        "lnb": W["rwkv_lnx_b"][l][None, :],
    }
    smp = {
        "cw": W["ssm_conv_w"][l],
        "cb": W["ssm_conv_b"][l][None, :],
        "dtb": _pad_cols(W["ssm_dt_bias"][l][None, :], LANES),
        "alog": _pad_cols(W["ssm_a_log"][l][None, :], LANES),
        "dvec": jnp.repeat(W["ssm_d"][l], SSM_HEAD_DIM)[None, :],
        "nrm": W["ssm_norm"][l][None, :],
    }
    lv = W["att_lambda"][l].astype(F32)
    lam_init = 0.8 - 0.6 * math.exp(-0.3 * l)
    lam = jnp.exp(jnp.sum(lv[0] * lv[1])) - jnp.exp(jnp.sum(lv[2] * lv[3])) + lam_init
    out.update(rwp=rwp, smp=smp, lam=lam, lam_init=lam_init)
    return out


def _layer(l, x, mod, lw, W, bsz, t, state, cfg, paged):
    n = bsz * t
    tm, tn = cfg["tm"], cfg["tn"]
    shift0, wkv0, conv0, ssm0 = state
    h = _norm(x, W["norm1"][l], mod, 1, 0, cfg["tm_n"])[0]
    mm = lambda w, tn_, dt=F32: _mm(h, w, l, tm, tn_, dt)
    qkv = mm(lw["w_qkv"], 512)
    rw = mm(lw["w_rw"], 384)
    sm = mm(lw["w_sm"], 640)
    gates = mm(lw["w_gate"], 512, BF16)
    if paged is None:
        o_att = _attn_prompt(qkv, W["rel_bias"], lw["lam"], W["att_subln"][l], lw["lam_init"], bsz, t, cfg["tq"])
    else:
        cache_k, cache_v, page_table = paged
        o_att = _attn_sample(qkv, cache_k, cache_v, page_table, l, W["rel_bias"], lw["lam"], W["att_subln"][l],
                             lw["lam_init"], bsz, t)
    prep = _rwkv_prep(rw, shift0, lw["rwp"], bsz, t, cfg["rw_tm"])
    o_rwkv, wkv1 = _rwkv_scan(prep, wkv0, lw["rwp"]["lng"], lw["rwp"]["lnb"], bsz, t, cfg["rw_bb"], cfg["rw_tc"],
                               cfg["rw_pg"])
    o_ssm, ssm1 = _mamba(sm, conv0, ssm0, lw["smp"], bsz, t)
    merged = _merge(o_att, o_rwkv, o_ssm, gates, lw["wb"], l, cfg["tm2"], tn)
    x = _proj_res(merged, lw["wo"], l, x, mod, 2, cfg["tm2"], tn)
    npad = -(-n // LANES) * LANES
    xq, modq, tmq = x, mod, tm
    if npad != n:
        xq = _pad_rows(x, npad)
        modq = jnp.pad(mod, ((0, 0), (0, npad - n), (0, 0)))
        tmq = npad
    h2, ht = _norm(xq, W["norm2"][l], modq, 4, 3, cfg["tmT"], with_transpose=True)
    q = _mm(h2, lw["wq"], l, tmq, tn)
    cnt, rank2, e1, e2 = _peer_route(q, W["peer_k1"][l], W["peer_k2"][l], cfg["tt_route"])
    out_t = _peer_dense(ht, lw["u"], lw["vt"], l, cnt, rank2, e1, e2, cfg["tt_dense"], cfg["ib"])
    x = _peer_res(xq, out_t, modq, 5, cfg["tmT"])[:n]
    qkv3 = qkv.reshape(bsz, t, 3 * ATT_QK)
    k_new = qkv3[:, :, ATT_QK:2 * ATT_QK].reshape(bsz, t, ATT_HEADS, 2 * ATT_DH)
    v_new = qkv3[:, :, 2 * ATT_QK:].reshape(bsz, t, ATT_HEADS, ATT_VD)
    shift1 = rw.reshape(bsz, t, RWKV_COLS_PAD)[:, -1, :RWKV_COLS]
    assert t >= SSM_CONV - 1
    conv1 = sm.reshape(bsz, t, SSM_COLS_PAD)[:, -(SSM_CONV - 1):, BRANCH_WIDTH:BRANCH_WIDTH + SSM_CONV_DIM]
    return x, (k_new, v_new, shift1, wkv1, conv1, ssm1)


def _wkv_to_kernel(s):
    b = s.shape[0]
    s = s.reshape(b, RWKV_HEADS // 2, 2, RWKV_DH, RWKV_DH)
    return jnp.transpose(s, (0, 1, 3, 2, 4)).reshape(b, RWKV_HEADS // 2, RWKV_DH, 2 * RWKV_DH)


def _wkv_from_kernel(s):
    b = s.shape[0]
    s = s.reshape(b, RWKV_HEADS // 2, RWKV_DH, 2, RWKV_DH)
    return jnp.transpose(s, (0, 1, 3, 2, 4)).reshape(b, RWKV_HEADS, RWKV_DH, RWKV_DH)


def _run_group(x, mod_of_layer, W, lws, bsz, t, states, cfg, paged_of_layer):
    xs = x.reshape(bsz * t, D_MODEL)
    outs = []
    for l in range(DEPTH):
        xs, new = _layer(l, xs, mod_of_layer(l), lws[l], W, bsz, t, states[l], cfg, paged_of_layer(l))
        outs.append(new)
    y = _final_norm(xs, W["norm_f"], cfg["tm_f"]).reshape(bsz, t, D_MODEL)
    k = jnp.stack([o[0] for o in outs], axis=1)
    v = jnp.stack([o[1] for o in outs], axis=1)
    shift = jnp.stack([o[2] for o in outs], axis=0)
    wkv = jnp.stack([_wkv_from_kernel(o[3]) for o in outs], axis=0)
    conv = jnp.stack([o[4] for o in outs], axis=0)
    ssm = jnp.stack([o[5].reshape(bsz, SSM_HEADS, SSM_HEAD_DIM, SSM_STATE) for o in outs], axis=0)
    return y, k, v, shift, wkv, conv, ssm


PROMPT_CFG = dict(tm=2048, tm_n=256, tn=512, tm2=1024, tq=512, rw_tm=256, rw_bb=2, rw_tc=128, rw_pg=8, tmT=256,
                  tt_route=512, tt_dense=512, ib=8, tm_f=512)


def _sample_cfg(n, t):
    npad = -(-n // LANES) * LANES
    return dict(tm=n, tm_n=n, tn=512, tm2=n, tq=None, rw_tm=t, rw_bb=2, rw_tc=t, rw_pg=8, tmT=npad,
                tt_route=npad, tt_dense=npad, ib=8, tm_f=n)


def kernel(x_prompt, x_sample, cache_k, cache_v, state_rwkv_shift, state_rwkv, state_conv, state_ssm, page_table, c_prompt, c_sample, w_ada, b_ada, norm1, norm2, norm_f, w_in, rel_bias, att_lambda, att_subln, rwkv_mu, rwkv_w0, rwkv_w2, rwkv_a0, rwkv_a2, rwkv_g2, rwkv_kk, rwkv_ka, rwkv_rk, rwkv_lnx_g, rwkv_lnx_b, ssm_conv_w, ssm_conv_b, ssm_dt_bias, ssm_a_log, ssm_d, ssm_norm, w_branch, w_o, peer_wq, peer_k1, peer_k2, peer_u, peer_v):
    W = dict(norm1=norm1, norm2=norm2, norm_f=norm_f, w_in=w_in, rel_bias=rel_bias, att_lambda=att_lambda,
             att_subln=att_subln, rwkv_mu=rwkv_mu, rwkv_w0=rwkv_w0, rwkv_w2=rwkv_w2, rwkv_a0=rwkv_a0,
             rwkv_a2=rwkv_a2, rwkv_g2=rwkv_g2, rwkv_kk=rwkv_kk, rwkv_ka=rwkv_ka, rwkv_rk=rwkv_rk,
             rwkv_lnx_g=rwkv_lnx_g, rwkv_lnx_b=rwkv_lnx_b, ssm_conv_w=ssm_conv_w, ssm_conv_b=ssm_conv_b,
             ssm_dt_bias=ssm_dt_bias, ssm_a_log=ssm_a_log, ssm_d=ssm_d, ssm_norm=ssm_norm, w_branch=w_branch,
             w_o=w_o, peer_wq=peer_wq, peer_k1=peer_k1, peer_k2=peer_k2, peer_u=peer_u, peer_v=peer_v)
    bp, tp, _ = x_prompt.shape
    bs, ts, _ = x_sample.shape
    nseq = bp + bs
    rows = -(-nseq // SUBLANES) * SUBLANES
    c_all = _pad_rows(jnp.concatenate([c_prompt, c_sample], axis=0), rows)
    mod = _ada(c_all, w_ada, b_ada)
    big = _big_weights(W)
    lws = [_layer_weights(l, W, big) for l in range(DEPTH)]
    npair = RWKV_HEADS // 2
    zero_state = (jnp.zeros((bp, RWKV_COLS_PAD), F32),
                  jnp.zeros((bp, npair, RWKV_DH, LANES), F32),
                  jnp.zeros((bp, SUBLANES, SSM_CONV_DIM), F32),
                  jnp.zeros((bp, SSM_HEADS // 2, LANES, SSM_STATE), F32))
    outs_p = _run_group(x_prompt, lambda l: mod[l, :bp].reshape(bp, 1, 6 * D_MODEL), W, lws, bp, tp,
                        [zero_state] * DEPTH, PROMPT_CFG, lambda l: None)
    states_s = []
    for l in range(DEPTH):
        states_s.append((
            _pad_cols(state_rwkv_shift[l], RWKV_COLS_PAD),
            _wkv_to_kernel(state_rwkv[l]),
            jnp.pad(state_conv[l], ((0, 0), (SUBLANES - (SSM_CONV - 1), 0), (0, 0))),
            state_ssm[l].reshape(bs, SSM_HEADS // 2, LANES, SSM_STATE),
        ))
    ns = bs * ts
    mod_s = lambda l: jnp.repeat(mod[l, bp:bp + bs], ts, axis=0).reshape(1, ns, 6 * D_MODEL)
    outs_s = _run_group(x_sample, mod_s, W, lws, bs, ts, states_s, _sample_cfg(ns, ts),
                        lambda l: (cache_k, cache_v, page_table))
    return (outs_p[0], outs_s[0], *outs_p[1:], *outs_s[1:])
```
